```python
import jax, jax.numpy as jnp
from jax import lax
import numpy as np

D_MODEL = 4096
BATCH = 4
SEQ = 2048
DEPTH = 2
DEC_BATCH = 128
DEC_SEQ = 4
PAST_LEN = 16384
PAGE_SIZE = 128

HEADS_PER_GROUP = D_MODEL // 256
SB_HEADS = HEADS_PER_GROUP
SB_DH = 64
MLA_HEADS = HEADS_PER_GROUP
MLA_NOPE = 128
MLA_ROPE = 64
MLA_VDIM = 128
Q_LORA = 768
KV_LORA = 256
MB_HEADS = HEADS_PER_GROUP
MB_DH = 64
MB_BLOCK = 256
MB_TOPK = 3
SB_WIDTH = SB_HEADS * SB_DH
MLA_WIDTH = MLA_HEADS * MLA_VDIM
MB_WIDTH = MB_HEADS * MB_DH
MIX_WIDTH = SB_WIDTH + MLA_WIDTH + MB_WIDTH
SB_Q0 = 0
SB_K0 = SB_Q0 + SB_WIDTH
SB_V0 = SB_K0 + SB_DH
MLA_QA0 = SB_V0 + SB_DH
MLA_KVA0 = MLA_QA0 + Q_LORA
MB_Q0 = MLA_KVA0 + KV_LORA + MLA_ROPE
MB_K0 = MB_Q0 + MB_WIDTH
MB_V0 = MB_K0 + MB_DH
IN_WIDTH = MB_V0 + MB_DH
D_FF = 11008
N_EXPERTS = 8
TOP_K = 2
D_EXPERT = 4096
N_DENSE = (DEPTH + 1) // 2
N_MOE = DEPTH // 2
Q_BLOCK = 128
ROPE_THETA = 10000.0
NORM_EPS = 1e-6

kernel_name = 'hybrid_stickbreak_mla_moba_decoder_step'


def rmsnorm(x, g):
    xf = x.astype(jnp.float32)
    y = xf * lax.rsqrt(jnp.mean(xf * xf, axis=-1, keepdims=True) + NORM_EPS)
    return (y * g.astype(jnp.float32)).astype(x.dtype)


def alibi_slopes():
    return 2.0 ** (-8.0 * jnp.arange(1, MB_HEADS + 1, dtype=jnp.float32) / MB_HEADS)


def rope_angles(pos):
    inv = ROPE_THETA ** (-jnp.arange(0, MLA_ROPE, 2, dtype=jnp.float32) / MLA_ROPE)
    ang = pos.astype(jnp.float32)[:, None] * inv[None, :]
    return jnp.cos(ang), jnp.sin(ang)


def apply_rope(x, cos, sin):
    half = MLA_ROPE // 2
    xf = x.astype(jnp.float32)
    x1, x2 = xf[..., :half], xf[..., half:]
    return jnp.concatenate([x1 * cos - x2 * sin, x1 * sin + x2 * cos], axis=-1).astype(x.dtype)


def gather_pages(pool, layer, page_table):
    g = pool[layer, page_table]
    return g.reshape(g.shape[0], g.shape[1] * g.shape[2], g.shape[3])


def query_blocks(fn, n_tokens):
    out = lax.map(fn, jnp.arange(n_tokens // Q_BLOCK, dtype=jnp.int32))
    out = jnp.moveaxis(out, 0, 1)
    return out.reshape((out.shape[0], n_tokens) + out.shape[3:])


def stick_breaking(q, k, v, q_pos, k_pos):
    z = jnp.einsum('bthd,bld->bhtl', q, k).astype(jnp.float32) * (SB_DH ** -0.5)
    causal = k_pos[None, :] < q_pos[:, None]
    log_1mb = jnp.where(causal, jax.nn.log_sigmoid(-z), 0.0)
    later = lax.cumsum(log_1mb, axis=3, reverse=True) - log_1mb
    w = jnp.where(causal, jnp.exp(jax.nn.log_sigmoid(z) + later), 0.0)
    return jnp.einsum('bhtl,bld->bthd', w.astype(v.dtype), v)


def mla_attention(q_lat, q_pe, ckv, kpe, q_pos, k_pos):
    s = (jnp.einsum('bthc,blc->bhtl', q_lat, ckv) + jnp.einsum('bthr,blr->bhtl', q_pe, kpe)).astype(jnp.float32)
    s = s * ((MLA_NOPE + MLA_ROPE) ** -0.5)
    s = jnp.where(k_pos[None, :] <= q_pos[:, None], s, -jnp.inf)
    p = jax.nn.softmax(s, axis=-1).astype(ckv.dtype)
    return jnp.einsum('bhtl,blc->bthc', p, ckv)


def moba_attention(q, q_pos, k_mean, k_blk, v_blk, k_own, v_own, own_pos):
    B, T, H, _ = q.shape
    slopes = alibi_slopes()
    scale = MB_DH ** -0.5
    q_blk = q_pos // MB_BLOCK
    own_ok = (own_pos[None, :] <= q_pos[:, None]) & (own_pos[None, :] >= (q_blk * MB_BLOCK)[:, None])
    own_dist = (q_pos[:, None] - own_pos[None, :]).astype(jnp.float32)
    s_own = jnp.einsum('bthd,bld->bthl', q, k_own).astype(jnp.float32) * scale - slopes[None, :, None] * own_dist[:, None, :]
    s_own = jnp.where(own_ok[None, :, None, :], s_own, -jnp.inf)
    n_blk = k_blk.shape[1]
    n_sel = min(MB_TOPK, n_blk)
    if n_sel == 0:
        p_own = jax.nn.softmax(s_own, axis=-1).astype(v_own.dtype)
        return jnp.einsum('bthl,bld->bthd', p_own, v_own)
    gate = jnp.einsum('bthd,bnd->bthn', q, k_mean).astype(jnp.float32)
    fully_past = jnp.arange(n_blk)[None, :] < q_blk[:, None]
    gate = jnp.where(fully_past[None, :, None, :], gate, -jnp.inf)
    _, sel = lax.top_k(gate, n_sel)
    sel_ok = sel < q_blk[None, :, None, None]
    bidx = jnp.arange(B)[:, None, None, None]
    k_sel = k_blk[bidx, sel]
    v_sel = v_blk[bidx, sel]
    sel_pos = sel[..., None] * MB_BLOCK + jnp.arange(MB_BLOCK)
    sel_dist = (q_pos[None, :, None, None, None] - sel_pos).astype(jnp.float32)
    s_sel = jnp.einsum('bthd,bthsjd->bthsj', q, k_sel).astype(jnp.float32) * scale - slopes[None, None, :, None, None] * sel_dist
    s_sel = jnp.where(sel_ok[..., None], s_sel, -jnp.inf).reshape(B, T, H, n_sel * MB_BLOCK)
    p = jax.nn.softmax(jnp.concatenate([s_sel, s_own], axis=-1), axis=-1).astype(v_own.dtype)
    p_sel = p[..., :n_sel * MB_BLOCK].reshape(B, T, H, n_sel, MB_BLOCK)
    return jnp.einsum('bthsj,bthsjd->bthd', p_sel, v_sel) + jnp.einsum('bthl,bld->bthd', p[..., n_sel * MB_BLOCK:], v_own)


def mixer_inputs(h, pos, w_in, q_a_norm, w_q_b, kv_a_norm, w_uk):
    B, T, _ = h.shape
    p = h @ w_in
    sb_q = p[..., SB_Q0:SB_K0].reshape(B, T, SB_HEADS, SB_DH)
    sb_k = p[..., SB_K0:SB_V0]
    sb_v = p[..., SB_V0:MLA_QA0]
    cos, sin = rope_angles(pos)
    cq = rmsnorm(p[..., MLA_QA0:MLA_KVA0], q_a_norm)
    qh = jnp.einsum('btq,qhe->bthe', cq, w_q_b)
    q_pe = apply_rope(qh[..., MLA_NOPE:], cos[:, None, :], sin[:, None, :])
    q_lat = jnp.einsum('bthn,chn->bthc', qh[..., :MLA_NOPE], w_uk)
    ckv = rmsnorm(p[..., MLA_KVA0:MLA_KVA0 + KV_LORA], kv_a_norm)
    kpe = apply_rope(p[..., MLA_KVA0 + KV_LORA:MB_Q0], cos, sin)
    mb_q = p[..., MB_Q0:MB_K0].reshape(B, T, MB_HEADS, MB_DH)
    mb_k = p[..., MB_K0:MB_V0]
    mb_v = p[..., MB_V0:IN_WIDTH]
    return sb_q, sb_k, sb_v, q_lat, q_pe, ckv, kpe, mb_q, mb_k, mb_v


def mixer_output(o_sb, o_lat, o_mb, w_uv, g_mix, w_out):
    B, T = o_sb.shape[:2]
    o_mla = jnp.einsum('bthc,chv->bthv', o_lat, w_uv)
    cat = jnp.concatenate([
        rmsnorm(o_sb.reshape(B, T, SB_WIDTH), g_mix[:SB_WIDTH]),
        rmsnorm(o_mla.reshape(B, T, MLA_WIDTH), g_mix[SB_WIDTH:SB_WIDTH + MLA_WIDTH]),
        rmsnorm(o_mb.reshape(B, T, MB_WIDTH), g_mix[SB_WIDTH + MLA_WIDTH:]),
    ], axis=-1)
    return cat @ w_out


def mixer_prompt(h, w_in, q_a_norm, w_q_b, kv_a_norm, w_uk, w_uv, g_mix, w_out):
    B, T, _ = h.shape
    pos = jnp.arange(T, dtype=jnp.int32)
    sb_q, sb_k, sb_v, q_lat, q_pe, ckv, kpe, mb_q, mb_k, mb_v = mixer_inputs(h, pos, w_in, q_a_norm, w_q_b, kv_a_norm, w_uk)

    def qslice(a, c):
        return lax.dynamic_slice_in_dim(a, c * Q_BLOCK, Q_BLOCK, axis=1)

    def qpos(c):
        return c * Q_BLOCK + jnp.arange(Q_BLOCK, dtype=jnp.int32)

    o_sb = query_blocks(lambda c: stick_breaking(qslice(sb_q, c), sb_k, sb_v, qpos(c), pos), T)
    o_lat = query_blocks(lambda c: mla_attention(qslice(q_lat, c), qslice(q_pe, c), ckv, kpe, qpos(c), pos), T)

    n_blk = T // MB_BLOCK
    k_blk = mb_k[:, :n_blk * MB_BLOCK].reshape(B, n_blk, MB_BLOCK, MB_DH)
    v_blk = mb_v[:, :n_blk * MB_BLOCK].reshape(B, n_blk, MB_BLOCK, MB_DH)
    k_mean = jnp.mean(k_blk.astype(jnp.float32), axis=2).astype(mb_k.dtype)
    k_pad = jnp.pad(mb_k, ((0, 0), (0, MB_BLOCK), (0, 0)))
    v_pad = jnp.pad(mb_v, ((0, 0), (0, MB_BLOCK), (0, 0)))

    def moba_block(c):
        start = (c * Q_BLOCK // MB_BLOCK) * MB_BLOCK
        k_own = lax.dynamic_slice_in_dim(k_pad, start, MB_BLOCK, axis=1)
        v_own = lax.dynamic_slice_in_dim(v_pad, start, MB_BLOCK, axis=1)
        own_pos = start + jnp.arange(MB_BLOCK, dtype=jnp.int32)
        return moba_attention(qslice(mb_q, c), qpos(c), k_mean, k_blk, v_blk, k_own, v_own, own_pos)

    o_mb = query_blocks(moba_block, T)
    y = mixer_output(o_sb, o_lat, o_mb, w_uv, g_mix, w_out)
    return y, (sb_k, sb_v, ckv, kpe, mb_k, mb_v)


def mixer_sample(h, past_sb_k, past_sb_v, past_ckv, past_kpe, past_mb_k, past_mb_v,
                 w_in, q_a_norm, w_q_b, kv_a_norm, w_uk, w_uv, g_mix, w_out):
    B, T, _ = h.shape
    P = past_sb_k.shape[1]
    L = P + T
    pos = P + jnp.arange(T, dtype=jnp.int32)
    k_pos = jnp.arange(L, dtype=jnp.int32)
    sb_q, sb_k, sb_v, q_lat, q_pe, ckv, kpe, mb_q, mb_k, mb_v = mixer_inputs(h, pos, w_in, q_a_norm, w_q_b, kv_a_norm, w_uk)

    o_sb = stick_breaking(sb_q, jnp.concatenate([past_sb_k, sb_k], 1), jnp.concatenate([past_sb_v, sb_v], 1), pos, k_pos)
    o_lat = mla_attention(q_lat, q_pe, jnp.concatenate([past_ckv, ckv], 1), jnp.concatenate([past_kpe, kpe], 1), pos, k_pos)

    k_all = jnp.concatenate([past_mb_k, mb_k], 1)
    v_all = jnp.concatenate([past_mb_v, mb_v], 1)
    n_blk = L // MB_BLOCK
    k_blk = k_all[:, :n_blk * MB_BLOCK].reshape(B, n_blk, MB_BLOCK, MB_DH)
    v_blk = v_all[:, :n_blk * MB_BLOCK].reshape(B, n_blk, MB_BLOCK, MB_DH)
    k_mean = jnp.mean(k_blk.astype(jnp.float32), axis=2).astype(k_all.dtype)
    own_start = (P // MB_BLOCK) * MB_BLOCK
    own_pos = jnp.arange(own_start, L, dtype=jnp.int32)
    o_mb = moba_attention(mb_q, pos, k_mean, k_blk, v_blk, k_all[:, own_start:], v_all[:, own_start:], own_pos)

    y = mixer_output(o_sb, o_lat, o_mb, w_uv, g_mix, w_out)
    return y, (sb_k, sb_v, ckv, kpe, mb_k, mb_v)


def swiglu(h, w_gate, w_up, w_down):
    return (jax.nn.silu(h @ w_gate) * (h @ w_up)) @ w_down


def moe_swiglu(h, w_router, w_gate, w_up, w_down):
    logits = (h @ w_router).astype(jnp.float32)
    top_val, top_idx = lax.top_k(logits, TOP_K)
    probs = jax.nn.softmax(top_val, axis=-1)
    gates = jnp.sum(jax.nn.one_hot(top_idx, N_EXPERTS, dtype=jnp.float32) * probs[..., None], axis=-2).astype(h.dtype)
    y = jnp.zeros_like(h)
    for e in range(N_EXPERTS):
        y = y + gates[..., e:e + 1] * swiglu(h, w_gate[e], w_up[e], w_down[e])
    return y


def channel_mixer(h, layer, w_ff_gate, w_ff_up, w_ff_down, w_router, w_ex_gate, w_ex_up, w_ex_down):
    i = layer // 2
    if layer % 2 == 0:
        return swiglu(h, w_ff_gate[i], w_ff_up[i], w_ff_down[i])
    return moe_swiglu(h, w_router[i], w_ex_gate[i], w_ex_up[i], w_ex_down[i])


def setup_inputs(seed: int = 0) -> dict:
    key = jax.random.key(seed)
    ks = jax.random.split(key, 32)
    f32 = jnp.float32
    n_pages = PAST_LEN // PAGE_SIZE
    n_pool = (DEC_BATCH * n_pages * 5) // 4

    def nrm(k, shape, scale=1.0):
        return jax.random.normal(k, shape, f32) * scale

    def gain(k, shape):
        return 1.0 + 0.01 * jax.random.normal(k, shape, f32)

    page_table = jax.random.permutation(ks[8], n_pool)[:DEC_BATCH * n_pages].reshape(DEC_BATCH, n_pages).astype(jnp.int32)
    return {
        'x_prompt': nrm(ks[0], (BATCH, SEQ, D_MODEL)),
        'x_sample': nrm(ks[1], (DEC_BATCH, DEC_SEQ, D_MODEL)),
        'cache_sb_k': nrm(ks[2], (DEPTH, n_pool, PAGE_SIZE, SB_DH)),
        'cache_sb_v': nrm(ks[3], (DEPTH, n_pool, PAGE_SIZE, SB_DH)),
        'cache_mla_ckv': nrm(ks[4], (DEPTH, n_pool, PAGE_SIZE, KV_LORA)),
        'cache_mla_kpe': nrm(ks[5], (DEPTH, n_pool, PAGE_SIZE, MLA_ROPE)),
        'cache_moba_k': nrm(ks[6], (DEPTH, n_pool, PAGE_SIZE, MB_DH)),
        'cache_moba_v': nrm(ks[7], (DEPTH, n_pool, PAGE_SIZE, MB_DH)),
        'page_table': page_table,
        'ln_mix': gain(ks[9], (DEPTH, D_MODEL)),
        'w_in': nrm(ks[10], (DEPTH, D_MODEL, IN_WIDTH), D_MODEL ** -0.5),
        'q_a_norm': gain(ks[11], (DEPTH, Q_LORA)),
        'w_q_b': nrm(ks[12], (DEPTH, Q_LORA, MLA_HEADS, MLA_NOPE + MLA_ROPE), Q_LORA ** -0.5),
        'kv_a_norm': gain(ks[13], (DEPTH, KV_LORA)),
        'w_kv_b': nrm(ks[14], (DEPTH, KV_LORA, MLA_HEADS, MLA_NOPE + MLA_VDIM), KV_LORA ** -0.5),
        'g_mix': gain(ks[15], (DEPTH, MIX_WIDTH)),
        'w_out': nrm(ks[16], (DEPTH, MIX_WIDTH, D_MODEL), MIX_WIDTH ** -0.5),
        'ln_ff': gain(ks[17], (DEPTH, D_MODEL)),
        'w_ff_gate': nrm(ks[18], (N_DENSE, D_MODEL, D_FF), D_MODEL ** -0.5),
        'w_ff_up': nrm(ks[19], (N_DENSE, D_MODEL, D_FF), D_MODEL ** -0.5),
        'w_ff_down': nrm(ks[20], (N_DENSE, D_FF, D_MODEL), D_FF ** -0.5),
        'w_router': nrm(ks[21], (N_MOE, D_MODEL, N_EXPERTS), D_MODEL ** -0.5),
        'w_ex_gate': nrm(ks[22], (N_MOE, N_EXPERTS, D_MODEL, D_EXPERT), D_MODEL ** -0.5),
        'w_ex_up': nrm(ks[23], (N_MOE, N_EXPERTS, D_MODEL, D_EXPERT), D_MODEL ** -0.5),
        'w_ex_down': nrm(ks[24], (N_MOE, N_EXPERTS, D_EXPERT, D_MODEL), D_EXPERT ** -0.5),
        'ln_final': gain(ks[25], (D_MODEL,)),
    }


def reference(x_prompt, x_sample, cache_sb_k, cache_sb_v, cache_mla_ckv, cache_mla_kpe, cache_moba_k, cache_moba_v,
              page_table, ln_mix, w_in, q_a_norm, w_q_b, kv_a_norm, w_kv_b, g_mix, w_out, ln_ff,
              w_ff_gate, w_ff_up, w_ff_down, w_router, w_ex_gate, w_ex_up, w_ex_down, ln_final):
    xp, xs = x_prompt, x_sample
    rows_p = [[] for _ in range(6)]
    rows_s = [[] for _ in range(6)]
    for l in range(DEPTH):
        w_uk = w_kv_b[l][..., :MLA_NOPE]
        w_uv = w_kv_b[l][..., MLA_NOPE:]
        mix_w = (w_in[l], q_a_norm[l], w_q_b[l], kv_a_norm[l], w_uk, w_uv, g_mix[l], w_out[l])
        yp, new_p = mixer_prompt(rmsnorm(xp, ln_mix[l]), *mix_w)
        past = (gather_pages(cache_sb_k, l, page_table), gather_pages(cache_sb_v, l, page_table),
                gather_pages(cache_mla_ckv, l, page_table), gather_pages(cache_mla_kpe, l, page_table),
                gather_pages(cache_moba_k, l, page_table), gather_pages(cache_moba_v, l, page_table))
        ys, new_s = mixer_sample(rmsnorm(xs, ln_mix[l]), *past, *mix_w)
        xp = xp + yp
        xs = xs + ys
        ffw = (w_ff_gate, w_ff_up, w_ff_down, w_router, w_ex_gate, w_ex_up, w_ex_down)
        xp = xp + channel_mixer(rmsnorm(xp, ln_ff[l]), l, *ffw)
        xs = xs + channel_mixer(rmsnorm(xs, ln_ff[l]), l, *ffw)
        for i in range(6):
            rows_p[i].append(new_p[i])
            rows_s[i].append(new_s[i])
    y_prompt = rmsnorm(xp, ln_final)
    y_sample = rmsnorm(xs, ln_final)
    new_sb_k_p, new_sb_v_p, new_ckv_p, new_kpe_p, new_mb_k_p, new_mb_v_p = [jnp.stack(r, 0) for r in rows_p]
    new_sb_k_s, new_sb_v_s, new_ckv_s, new_kpe_s, new_mb_k_s, new_mb_v_s = [jnp.stack(r, 0) for r in rows_s]
    return (y_prompt, y_sample,
            new_sb_k_p, new_sb_v_p, new_ckv_p, new_kpe_p, new_mb_k_p, new_mb_v_p,
            new_sb_k_s, new_sb_v_s, new_ckv_s, new_kpe_s, new_mb_k_s, new_mb_v_s)
```

```python
import functools
import math

import jax
import jax.numpy as jnp
from jax import lax
from jax.experimental import pallas as pl
from jax.experimental.pallas import tpu as pltpu

MB_BLOCK = 256
MB_TOPK = 3
TOP_K = 2
ROPE_THETA = 10000.0
NORM_EPS = 1e-6

V7X_VMEM_LIMIT_BYTES = 56 * 1024 * 1024
LANES = 128
NEG = -1e30
NEW_PAD = 128

bf16 = jnp.bfloat16
f32 = jnp.float32


def _cp(*sem):
    return pltpu.CompilerParams(dimension_semantics=sem, vmem_limit_bytes=V7X_VMEM_LIMIT_BYTES)


def _pick(n, prefs):
    for p in prefs:
        if n % p == 0:
            return p
    return n


def _pick_k(k):
    if k <= 4096:
        return k
    for d in range(2, 65):
        if k % d == 0 and (k // d) % LANES == 0 and k // d <= 6144:
            return k // d
    return k


def _dot(a, b):
    return jnp.dot(a, b, preferred_element_type=f32)


def _dot_t(a, b):
    return lax.dot_general(a, b, (((1,), (1,)), ((), ())), preferred_element_type=f32)


def _split(x):
    hi = x.astype(bf16)
    lo = (x - hi.astype(f32)).astype(bf16)
    return hi, lo


def _rms(x, g):
    return x * lax.rsqrt(jnp.mean(x * x, axis=-1, keepdims=True) + NORM_EPS) * g


def _rmsnorm_kernel(x_ref, g_ref, o_ref):
    o_ref[...] = _rms(x_ref[...].astype(f32), g_ref[...]).astype(o_ref.dtype)


def rmsnorm(x, g, out_dtype):
    m, d = x.shape
    tm = _pick(m, (512, 256, 128, 64, 32, 16))
    return pl.pallas_call(
        _rmsnorm_kernel,
        grid=(m // tm,),
        in_specs=[pl.BlockSpec((tm, d), lambda i: (i, 0)), pl.BlockSpec((1, d), lambda i: (0, 0))],
        out_specs=pl.BlockSpec((tm, d), lambda i: (i, 0)),
        out_shape=jax.ShapeDtypeStruct((m, d), out_dtype),
        compiler_params=_cp("parallel"),
    )(x, g.reshape(1, d).astype(f32))


def _mm_kernel(*refs, nk, has_res):
    if has_res:
        a_ref, w_ref, r_ref, o_ref = refs[:4]
    else:
        a_ref, w_ref, o_ref = refs[:3]
        r_ref = None
    part = _dot(a_ref[...], w_ref[...])

    def finish(v):
        if r_ref is not None:
            v = v + r_ref[...]
        o_ref[...] = v.astype(o_ref.dtype)

    if nk == 1:
        finish(part)
        return
    acc = refs[-1]
    k = pl.program_id(2)

    @pl.when(k == 0)
    def _():
        acc[...] = part

    @pl.when(k > 0)
    def _():
        acc[...] += part

    @pl.when(k == nk - 1)
    def _():
        finish(acc[...])


def matmul(a, w, residual=None, out_dtype=f32):
    m, kdim = a.shape
    n = w.shape[1]
    tm = _pick(m, (512, 256, 128, 64, 32, 16))
    tn = _pick(n, (512, 256, 128))
    tk = _pick_k(kdim)
    nk = kdim // tk
    in_specs = [pl.BlockSpec((tm, tk), lambda i, j, k: (i, k)), pl.BlockSpec((tk, tn), lambda i, j, k: (k, j))]
    args = [a, w]
    if residual is not None:
        in_specs.append(pl.BlockSpec((tm, tn), lambda i, j, k: (i, j)))
        args.append(residual)
    scratch = [pltpu.VMEM((tm, tn), f32)] if nk > 1 else []
    return pl.pallas_call(
        functools.partial(_mm_kernel, nk=nk, has_res=residual is not None),
        grid=(m // tm, n // tn, nk),
        in_specs=in_specs,
        out_specs=pl.BlockSpec((tm, tn), lambda i, j, k: (i, j)),
        out_shape=jax.ShapeDtypeStruct((m, n), out_dtype),
        scratch_shapes=scratch,
        compiler_params=_cp("parallel", "parallel", "arbitrary"),
    )(*args)


def _gateup_kernel(*refs, has_scale):
    if has_scale:
        h_ref, wg_ref, wu_ref, s_ref, o_ref = refs
    else:
        h_ref, wg_ref, wu_ref, o_ref = refs
    h = h_ref[...]
    g = _dot(h, wg_ref[...])
    u = _dot(h, wu_ref[...])
    a = g / (1.0 + jnp.exp(-g)) * u
    if has_scale:
        a = a * s_ref[...]
    o_ref[...] = a.astype(o_ref.dtype)


def gateup(h, wg, wu, row_scale=None):
    m, d = h.shape
    e, _, f = wg.shape
    tm = _pick(m, (512, 256, 128, 64, 32, 16))
    tn = _pick(f, (512, 256, 128))
    nf = f // tn
    w_spec = pl.BlockSpec((None, d, tn), lambda j, i: (j // nf, 0, j % nf))
    in_specs = [pl.BlockSpec((tm, d), lambda j, i: (i, 0)), w_spec, w_spec]
    args = [h, wg, wu]
    if row_scale is not None:
        in_specs.append(pl.BlockSpec((None, tm, 1), lambda j, i: (j // nf, i, 0)))
        args.append(row_scale)
    return pl.pallas_call(
        functools.partial(_gateup_kernel, has_scale=row_scale is not None),
        grid=(e * nf, m // tm),
        in_specs=in_specs,
        out_specs=pl.BlockSpec((tm, tn), lambda j, i: (i, j)),
        out_shape=jax.ShapeDtypeStruct((m, e * f), bf16),
        compiler_params=_cp("parallel", "parallel"),
    )(*args)


def _postproj_kernel(p_ref, cos_ref, sin_ref, qan_ref, kvan_ref, wqn_ref, wqr_ref, wukt_ref,
                     sbq_ref, mbq_ref, qlat_ref, qpe_ref, sbk_ref, sbv_ref, ckv_ref, kpe_ref, mbk_ref, mbv_ref,
                     *, dims):
    H, sb_dh, mb_dh, ql, kvl, rope, nope = (dims[k] for k in ("H", "sb_dh", "mb_dh", "ql", "kvl", "rope", "nope"))
    sb_q0 = 0
    sb_k0 = sb_q0 + H * sb_dh
    sb_v0 = sb_k0 + sb_dh
    qa0 = sb_v0 + sb_dh
    kva0 = qa0 + ql
    kpe0 = kva0 + kvl
    mb_q0 = kpe0 + rope
    mb_k0 = mb_q0 + H * mb_dh
    mb_v0 = mb_k0 + mb_dh
    half = rope // 2
    cos = cos_ref[...]
    sin = sin_ref[...]

    def rot(x):
        return x * cos + jnp.concatenate([x[:, half:], x[:, :half]], axis=-1) * sin

    sb_scale = sb_dh ** -0.5
    mb_scale = mb_dh ** -0.5
    for h in range(H):
        sbq_ref[h] = (p_ref[:, sb_q0 + h * sb_dh:sb_q0 + (h + 1) * sb_dh] * sb_scale).astype(bf16)
        mbq_ref[h] = (p_ref[:, mb_q0 + h * mb_dh:mb_q0 + (h + 1) * mb_dh] * mb_scale).astype(bf16)
    sbk_ref[...] = p_ref[:, sb_k0:sb_k0 + sb_dh]
    sbv_ref[...] = p_ref[:, sb_v0:sb_v0 + sb_dh]
    mbk_ref[...] = p_ref[:, mb_k0:mb_k0 + mb_dh]
    mbv_ref[...] = p_ref[:, mb_v0:mb_v0 + mb_dh]
    ckv_ref[...] = _rms(p_ref[:, kva0:kva0 + kvl], kvan_ref[...])
    kpe_ref[...] = rot(p_ref[:, kpe0:kpe0 + rope])
    cq = _rms(p_ref[:, qa0:qa0 + ql], qan_ref[...]).astype(bf16)
    qn = _dot(cq, wqn_ref[...])
    qr = _dot(cq, wqr_ref[...])
    for h in range(H):
        qlat_ref[h] = _dot(qn[:, h * nope:(h + 1) * nope].astype(bf16), wukt_ref[h]).astype(bf16)
        qpe_ref[h] = rot(qr[:, h * rope:(h + 1) * rope]).astype(bf16)


def postproj(p, cos64, sin64, q_a_norm, kv_a_norm, w_qn, w_qr, w_ukt, dims):
    m, npad = p.shape
    H, sb_dh, mb_dh, ql, kvl, rope, nope = (dims[k] for k in ("H", "sb_dh", "mb_dh", "ql", "kvl", "rope", "nope"))
    tm = _pick(m, (256, 128, 64, 32, 16))
    row = lambda w: pl.BlockSpec((tm, w), lambda i: (i, 0))
    full = lambda a: pl.BlockSpec(a.shape, lambda i: (0,) * a.ndim)
    hd = lambda w: pl.BlockSpec((H, tm, w), lambda i: (0, i, 0))
    qan = q_a_norm.reshape(1, ql).astype(f32)
    kvan = kv_a_norm.reshape(1, kvl).astype(f32)
    out_shape = (
        jax.ShapeDtypeStruct((H, m, sb_dh), bf16), jax.ShapeDtypeStruct((H, m, mb_dh), bf16),
        jax.ShapeDtypeStruct((H, m, kvl), bf16), jax.ShapeDtypeStruct((H, m, rope), bf16),
        jax.ShapeDtypeStruct((m, sb_dh), f32), jax.ShapeDtypeStruct((m, sb_dh), f32),
        jax.ShapeDtypeStruct((m, kvl), f32), jax.ShapeDtypeStruct((m, rope), f32),
        jax.ShapeDtypeStruct((m, mb_dh), f32), jax.ShapeDtypeStruct((m, mb_dh), f32),
    )
    out_specs = (hd(sb_dh), hd(mb_dh), hd(kvl), hd(rope), row(sb_dh), row(sb_dh), row(kvl), row(rope), row(mb_dh), row(mb_dh))
    return pl.pallas_call(
        functools.partial(_postproj_kernel, dims=dims),
        grid=(m // tm,),
        in_specs=[row(npad), row(rope), row(rope), full(qan), full(kvan), full(w_qn), full(w_qr), full(w_ukt)],
        out_specs=out_specs,
        out_shape=out_shape,
        compiler_params=_cp("parallel"),
    )(p, cos64, sin64, qan, kvan, w_qn, w_qr, w_ukt)


def _mixnorm_kernel(osb_ref, olat_ref, omb_ref, wuv_ref, g_ref, o_ref, omla_sc, *, H, vdim):
    sbw = osb_ref.shape[1]
    mbw = omb_ref.shape[1]
    mlaw = H * vdim
    for h in range(H):
        omla_sc[:, h * vdim:(h + 1) * vdim] = _dot(olat_ref[h], wuv_ref[h])
    o_ref[:, :sbw] = _rms(osb_ref[...], g_ref[:, :sbw]).astype(bf16)
    o_ref[:, sbw:sbw + mlaw] = _rms(omla_sc[...], g_ref[:, sbw:sbw + mlaw]).astype(bf16)
    o_ref[:, sbw + mlaw:] = _rms(omb_ref[...], g_ref[:, sbw + mlaw:sbw + mlaw + mbw]).astype(bf16)


def mixnorm(o_sb, o_lat, o_mb, w_uv, g_mix):
    m, sbw = o_sb.shape
    H, _, kvl = o_lat.shape
    vdim = w_uv.shape[2]
    mbw = o_mb.shape[1]
    mix = sbw + H * vdim + mbw
    tm = _pick(m, (256, 128, 64, 32, 16))
    return pl.pallas_call(
        functools.partial(_mixnorm_kernel, H=H, vdim=vdim),
        grid=(m // tm,),
        in_specs=[pl.BlockSpec((tm, sbw), lambda i: (i, 0)), pl.BlockSpec((H, tm, kvl), lambda i: (0, i, 0)),
                  pl.BlockSpec((tm, mbw), lambda i: (i, 0)), pl.BlockSpec(w_uv.shape, lambda i: (0, 0, 0)),
                  pl.BlockSpec((1, mix), lambda i: (0, 0))],
        out_specs=pl.BlockSpec((tm, mix), lambda i: (i, 0)),
        out_shape=jax.ShapeDtypeStruct((m, mix), bf16),
        scratch_shapes=[pltpu.VMEM((tm, H * vdim), f32)],
        compiler_params=_cp("parallel"),
    )(o_sb, o_lat, o_mb, w_uv, g_mix.reshape(1, mix).astype(f32))


def _router_kernel(x_ref, g_ref, wr_ref, h_ref, gates_ref, *, n_exp):
    h = _rms(x_ref[...], g_ref[...])
    h_ref[...] = h.astype(bf16)
    h_hi, h_lo = _split(h)
    w_hi, w_lo = _split(wr_ref[...])
    logits = _dot(h_hi, w_hi) + (_dot(h_hi, w_lo) + _dot(h_lo, w_hi))
    lane = lax.broadcasted_iota(jnp.int32, logits.shape, 1)
    work = jnp.where(lane < n_exp, logits, -jnp.inf)
    vals, idxs = [], []
    for _ in range(TOP_K):
        mx = jnp.max(work, axis=1, keepdims=True)
        ix = jnp.min(jnp.where(work == mx, lane, LANES), axis=1, keepdims=True)
        vals.append(mx)
        idxs.append(ix)
        work = jnp.where(lane == ix, -jnp.inf, work)
    ex = [jnp.exp(v - vals[0]) for v in vals]
    den = ex[0]
    for e_ in ex[1:]:
        den = den + e_
    gates = jnp.zeros_like(logits)
    for e_, ix in zip(ex, idxs):
        gates = gates + jnp.where(lane == ix, e_ / den, 0.0)
    gates_ref[...] = gates


def router(x, g, w_router):
    m, d = x.shape
    n_exp = w_router.shape[1]
    wr = jnp.pad(w_router.astype(f32), ((0, 0), (0, LANES - n_exp)))
    tm = _pick(m, (256, 128, 64, 32, 16))
    return pl.pallas_call(
        functools.partial(_router_kernel, n_exp=n_exp),
        grid=(m // tm,),
        in_specs=[pl.BlockSpec((tm, d), lambda i: (i, 0)), pl.BlockSpec((1, d), lambda i: (0, 0)),
                  pl.BlockSpec((d, LANES), lambda i: (0, 0))],
        out_specs=(pl.BlockSpec((tm, d), lambda i: (i, 0)), pl.BlockSpec((tm, LANES), lambda i: (i, 0))),
        out_shape=(jax.ShapeDtypeStruct((m, d), bf16), jax.ShapeDtypeStruct((m, LANES), f32)),
        compiler_params=_cp("parallel"),
    )(x, g.reshape(1, d).astype(f32), wr)


def _upper(tk):
    j = lax.broadcasted_iota(jnp.int32, (tk, tk), 0)
    s = lax.broadcasted_iota(jnp.int32, (tk, tk), 1)
    return jnp.where(j > s, 1.0, 0.0).astype(bf16)


def _sb_block(q, k, v, causal, carry, acc):
    rows = q.shape[0]
    tk = k.shape[0]
    z = _dot_t(q, k)
    sp = jnp.maximum(z, 0.0) + jnp.log1p(jnp.exp(-jnp.abs(z)))
    l1 = -sp if causal is None else jnp.where(causal, -sp, 0.0)
    hi, lo = _split(l1)
    cum = _dot(jnp.concatenate([hi, lo], axis=0), _upper(tk))
    later = carry + (cum[:rows] + cum[rows:])
    w = jnp.exp(z - sp + later)
    if causal is not None:
        w = jnp.where(causal, w, 0.0)
    acc = acc + _dot(w.astype(bf16), v)
    carry = carry + jnp.sum(l1, axis=1, keepdims=True)
    return carry, acc


def _sb_prompt_kernel(q_ref, k_ref, v_ref, o_ref, carry_sc, acc_sc, *, H, tq, tk):
    qi = pl.program_id(1)
    st = pl.program_id(2)
    dh = q_ref.shape[2]
    rows = H * tq
    kb = (qi * tq + tq - 1) // tk - st

    @pl.when(st == 0)
    def _():
        carry_sc[...] = jnp.zeros_like(carry_sc)
        acc_sc[...] = jnp.zeros_like(acc_sc)

    @pl.when(kb >= 0)
    def _():
        q = q_ref[...].reshape(rows, dh)
        qpos = qi * tq + lax.broadcasted_iota(jnp.int32, (rows, tk), 0) % tq
        kpos = kb * tk + lax.broadcasted_iota(jnp.int32, (rows, tk), 1)
        carry, acc = _sb_block(q, k_ref[...].astype(bf16), v_ref[...].astype(bf16), kpos < qpos,
                               carry_sc[...], acc_sc[...])
        carry_sc[...] = carry
        acc_sc[...] = acc

    @pl.when(st == pl.num_programs(2) - 1)
    def _():
        for h in range(H):
            o_ref[:, h * dh:(h + 1) * dh] = acc_sc[h * tq:(h + 1) * tq, :]


def sb_prompt(q, k, v, B, T):
    H, _, dh = q.shape
    tq = _pick(T, (128, 64, 32, 16))
    tk = _pick(T, (256, 128, 64, 32, 16))
    nq, nk = T // tq, T // tk

    def kv_map(b, qi, st):
        return (b * nk + jnp.maximum((qi * tq + tq - 1) // tk - st, 0), 0)

    return pl.pallas_call(
        functools.partial(_sb_prompt_kernel, H=H, tq=tq, tk=tk),
        grid=(B, nq, nk),
        in_specs=[pl.BlockSpec((H, tq, dh), lambda b, qi, st: (0, b * nq + qi, 0)),
                  pl.BlockSpec((tk, dh), kv_map), pl.BlockSpec((tk, dh), kv_map)],
        out_specs=pl.BlockSpec((tq, H * dh), lambda b, qi, st: (b * nq + qi, 0)),
        out_shape=jax.ShapeDtypeStruct((B * T, H * dh), f32),
        scratch_shapes=[pltpu.VMEM((H * tq, 1), f32), pltpu.VMEM((H * tq, dh), f32)],
        compiler_params=_cp("parallel", "parallel", "arbitrary"),
    )(q, k, v)


def _paged_start(pt_ref, seq, first_page, n_pages, page, layer, pairs, slot):
    def body(j, _):
        pg = pt_ref[seq, first_page + j]
        for cache, buf, sem in pairs:
            pltpu.make_async_copy(cache.at[layer, pg], buf.at[slot, pl.ds(j * page, page), :], sem.at[slot]).start()
        return 0
    lax.fori_loop(0, n_pages, body, 0)


def _paged_wait(n_pages, page, layer, pairs, slot):
    def body(j, _):
        for cache, buf, sem in pairs:
            pltpu.make_async_copy(cache.at[layer, 0], buf.at[slot, pl.ds(j * page, page), :], sem.at[slot]).wait()
        return 0
    lax.fori_loop(0, n_pages, body, 0)


def _sb_sample_kernel(pt_ref, q_ref, kn_ref, vn_ref, kc_hbm, vc_hbm, o_ref, kbuf, vbuf, ksem, vsem,
                      *, layer, n_pages, page, DT, tk):
    b = pl.program_id(0)
    nb = pl.num_programs(0)
    slot = b % 2
    pairs = [(kc_hbm, kbuf, ksem), (vc_hbm, vbuf, vsem)]

    @pl.when(b == 0)
    def _():
        _paged_start(pt_ref, 0, 0, n_pages, page, layer, pairs, 0)

    @pl.when(b + 1 < nb)
    def _():
        _paged_start(pt_ref, b + 1, 0, n_pages, page, layer, pairs, 1 - slot)

    _paged_wait(n_pages, page, layer, pairs, slot)

    q = q_ref[0]
    rows, dh = q.shape
    t_idx = lax.broadcasted_iota(jnp.int32, (rows, NEW_PAD), 0) % DT
    j_idx = lax.broadcasted_iota(jnp.int32, (rows, NEW_PAD), 1)
    carry, acc = _sb_block(q, kn_ref[0].astype(bf16), vn_ref[0].astype(bf16), j_idx < t_idx,
                           jnp.zeros((rows, 1), f32), jnp.zeros((rows, dh), f32))
    n_grp = (n_pages * page) // tk

    def body(i, ca):
        off = pl.multiple_of((n_grp - 1 - i) * tk, tk)
        k = kbuf[slot, pl.ds(off, tk), :].astype(bf16)
        v = vbuf[slot, pl.ds(off, tk), :].astype(bf16)
        return _sb_block(q, k, v, None, ca[0], ca[1])

    carry, acc = lax.fori_loop(0, n_grp, body, (carry, acc))
    o_ref[0] = acc


def _sample_call(kernel, pt, blocked, caches, out_shape, out_block, scratch, grid, n_prefetch=1):
    nd = len(grid)
    def imap(shape):
        if nd == 1:
            return lambda b, pt_: (b,) + (0,) * (len(shape) - 1)
        return lambda b, c, pt_: (b,) + (0,) * (len(shape) - 1)
    in_specs = [pl.BlockSpec((1,) + a.shape[1:], imap(a.shape)) for a in blocked]
    in_specs += [pl.BlockSpec(memory_space=pl.ANY) for _ in caches]
    return pl.pallas_call(
        kernel,
        grid_spec=pltpu.PrefetchScalarGridSpec(
            num_scalar_prefetch=n_prefetch, grid=grid, in_specs=in_specs,
            out_specs=pl.BlockSpec(out_block, imap(out_shape.shape)), scratch_shapes=scratch),
        out_shape=out_shape,
        compiler_params=_cp(*(("arbitrary",) * nd)),
    )(pt, *blocked, *caches)


def sb_sample(q, k_new, v_new, cache_k, cache_v, page_table, layer, DT):
    DB, rows, dh = q.shape
    n_pages = page_table.shape[1]
    page = cache_k.shape[2]
    tk = 2 * page
    assert n_pages % 2 == 0
    P = n_pages * page
    scratch = [pltpu.VMEM((2, P, dh), f32), pltpu.VMEM((2, P, dh), f32),
               pltpu.SemaphoreType.DMA((2,)), pltpu.SemaphoreType.DMA((2,))]
    return _sample_call(
        functools.partial(_sb_sample_kernel, layer=layer, n_pages=n_pages, page=page, DT=DT, tk=tk),
        page_table, [q, k_new, v_new], [cache_k, cache_v],
        jax.ShapeDtypeStruct((DB, rows, dh), f32), (1, rows, dh), scratch, (DB,))


def _softmax_step(s, v, m_prev, l_prev, acc_prev, mask=None):
    if mask is not None:
        s = jnp.where(mask, s, NEG)
    m_new = jnp.maximum(m_prev, jnp.max(s, axis=1, keepdims=True))
    alpha = jnp.exp(m_prev - m_new)
    p = jnp.exp(s - m_new)
    if mask is not None:
        p = jnp.where(mask, p, 0.0)
    l_new = alpha * l_prev + jnp.sum(p, axis=1, keepdims=True)
    acc_new = alpha * acc_prev + _dot(p.astype(bf16), v)
    return m_new, l_new, acc_new


def _mla_prompt_kernel(ql_ref, qp_ref, ckv_ref, kpe_ref, o_ref, m_sc, l_sc, acc_sc, *, H, tq, tk, scale):
    qi = pl.program_id(1)
    kj = pl.program_id(2)
    rows = H * tq
    last = (qi * tq + tq - 1) // tk

    @pl.when(kj == 0)
    def _():
        m_sc[...] = jnp.full_like(m_sc, NEG)
        l_sc[...] = jnp.zeros_like(l_sc)
        acc_sc[...] = jnp.zeros_like(acc_sc)

    @pl.when(kj <= last)
    def _():
        ql = ql_ref[...].reshape(rows, ql_ref.shape[2])
        qp = qp_ref[...].reshape(rows, qp_ref.shape[2])
        ck = ckv_ref[...].astype(bf16)
        kp = kpe_ref[...].astype(bf16)
        s = (_dot_t(ql, ck) + _dot_t(qp, kp)) * scale
        qpos = qi * tq + lax.broadcasted_iota(jnp.int32, (rows, tk), 0) % tq
        kpos = kj * tk + lax.broadcasted_iota(jnp.int32, (rows, tk), 1)
        m, l, acc = _softmax_step(s, ck, m_sc[...], l_sc[...], acc_sc[...], kpos <= qpos)
        m_sc[...] = m
        l_sc[...] = l
        acc_sc[...] = acc

    @pl.when(kj == pl.num_programs(2) - 1)
    def _():
        o_ref[...] = (acc_sc[...] / l_sc[...]).reshape(o_ref.shape).astype(o_ref.dtype)


def mla_prompt(q_lat, q_pe, ckv, kpe, B, T, scale):
    H, _, C = q_lat.shape
    R = q_pe.shape[2]
    tq = _pick(T, (128, 64, 32, 16))
    tk = _pick(T, (512, 256, 128, 64, 32, 16))
    nq, nk = T // tq, T // tk

    def kv_map(b, qi, kj):
        return (b * nk + jnp.minimum(kj, (qi * tq + tq - 1) // tk), 0)

    q_map = lambda b, qi, kj: (0, b * nq + qi, 0)
    return pl.pallas_call(
        functools.partial(_mla_prompt_kernel, H=H, tq=tq, tk=tk, scale=scale),
        grid=(B, nq, nk),
        in_specs=[pl.BlockSpec((H, tq, C), q_map), pl.BlockSpec((H, tq, R), q_map),
                  pl.BlockSpec((tk, C), kv_map), pl.BlockSpec((tk, R), kv_map)],
        out_specs=pl.BlockSpec((H, tq, C), q_map),
        out_shape=jax.ShapeDtypeStruct((H, B * T, C), bf16),
        scratch_shapes=[pltpu.VMEM((H * tq, 1), f32), pltpu.VMEM((H * tq, 1), f32), pltpu.VMEM((H * tq, C), f32)],
        compiler_params=_cp("parallel", "parallel", "arbitrary"),
    )(q_lat, q_pe, ckv, kpe)


def _mla_sample_kernel(pt_ref, ql_ref, qp_ref, cn_ref, pn_ref, ckv_hbm, kpe_hbm, o_ref,
                       cbuf, pbuf, csem, psem, m_sc, l_sc, acc_sc, *, layer, chunk, page, DT, scale):
    b = pl.program_id(0)
    c = pl.program_id(1)
    nc = pl.num_programs(1)
    step = b * nc + c
    total = pl.num_programs(0) * nc
    slot = step % 2
    pairs = [(ckv_hbm, cbuf, csem), (kpe_hbm, pbuf, psem)]

    @pl.when(step == 0)
    def _():
        _paged_start(pt_ref, 0, 0, chunk, page, layer, pairs, 0)

    @pl.when(step + 1 < total)
    def _():
        nxt = step + 1
        _paged_start(pt_ref, nxt // nc, (nxt % nc) * chunk, chunk, page, layer, pairs, 1 - slot)

    _paged_wait(chunk, page, layer, pairs, slot)

    ql = ql_ref[0]
    qp = qp_ref[0]
    rows = ql.shape[0]

    @pl.when(c == 0)
    def _():
        cn = cn_ref[0].astype(bf16)
        pn = pn_ref[0].astype(bf16)
        s = (_dot_t(ql, cn) + _dot_t(qp, pn)) * scale
        t_idx = lax.broadcasted_iota(jnp.int32, (rows, NEW_PAD), 0) % DT
        j_idx = lax.broadcasted_iota(jnp.int32, (rows, NEW_PAD), 1)
        m, l, acc = _softmax_step(s, cn, jnp.full((rows, 1), NEG, f32), jnp.zeros((rows, 1), f32),
                                  jnp.zeros(acc_sc.shape, f32), j_idx <= t_idx)
        m_sc[...] = m
        l_sc[...] = l
        acc_sc[...] = acc

    ck = cbuf[slot].astype(bf16)
    kp = pbuf[slot].astype(bf16)
    s = (_dot_t(ql, ck) + _dot_t(qp, kp)) * scale
    m, l, acc = _softmax_step(s, ck, m_sc[...], l_sc[...], acc_sc[...])
    m_sc[...] = m
    l_sc[...] = l
    acc_sc[...] = acc

    @pl.when(c == nc - 1)
    def _():
        o_ref[0] = (acc_sc[...] / l_sc[...]).astype(o_ref.dtype)


def mla_sample(q_lat, q_pe, ckv_new, kpe_new, cache_ckv, cache_kpe, page_table, layer, DT, scale):
    DB, rows, C = q_lat.shape
    R = q_pe.shape[2]
    n_pages = page_table.shape[1]
    page = cache_ckv.shape[2]
    chunk = _pick(n_pages, (32, 16, 8, 4, 2, 1))
    scratch = [pltpu.VMEM((2, chunk * page, C), f32), pltpu.VMEM((2, chunk * page, R), f32),
               pltpu.SemaphoreType.DMA((2,)), pltpu.SemaphoreType.DMA((2,)),
               pltpu.VMEM((rows, 1), f32), pltpu.VMEM((rows, 1), f32), pltpu.VMEM((rows, C), f32)]
    return _sample_call(
        functools.partial(_mla_sample_kernel, layer=layer, chunk=chunk, page=page, DT=DT, scale=scale),
        page_table, [q_lat, q_pe, ckv_new, kpe_new], [cache_ckv, cache_kpe],
        jax.ShapeDtypeStruct((DB, rows, C), bf16), (1, rows, C), scratch, (DB, n_pages // chunk))


def _moba_select(q, kmean, n_valid):
    nb, dh = kmean.shape
    km = jnp.concatenate([kmean, jnp.zeros((LANES - nb, dh), f32)], axis=0) if nb < LANES else kmean
    k_hi, k_lo = _split(km)
    gate = _dot_t(q, k_hi) + _dot_t(q, k_lo)
    blk = lax.broadcasted_iota(jnp.int32, gate.shape, 1)
    valid = blk < n_valid
    work = jnp.where(valid, gate, -jnp.inf)
    sel = jnp.zeros(gate.shape, jnp.bool_)
    for _ in range(MB_TOPK):
        mx = jnp.max(work, axis=1, keepdims=True)
        ix = jnp.min(jnp.where(work == mx, blk, LANES), axis=1, keepdims=True)
        pick = blk == ix
        sel = jnp.logical_or(sel, pick)
        work = jnp.where(pick, -jnp.inf, work)
    return jnp.logical_and(sel, valid)


def _block_means(k, nb):
    return jnp.mean(k.reshape(nb, MB_BLOCK, k.shape[1]), axis=1)


def _moba_prompt_kernel(q_ref, k_ref, v_ref, slope_ref, o_ref, m_sc, l_sc, acc_sc, *, H, tq, nb):
    qi = pl.program_id(1)
    dh = q_ref.shape[2]
    rows = H * tq
    q = q_ref[...].reshape(rows, dh)
    q_blk = (qi * tq) // MB_BLOCK
    sel = _moba_select(q, _block_means(k_ref[...], nb), q_blk)
    sel_f = jnp.where(sel, 1.0, 0.0)
    blk = lax.broadcasted_iota(jnp.int32, sel.shape, 1)
    slope = slope_ref[...]
    qpos = qi * tq + lax.broadcasted_iota(jnp.int32, (rows, MB_BLOCK), 0) % tq
    m_sc[...] = jnp.full_like(m_sc, NEG)
    l_sc[...] = jnp.zeros_like(l_sc)
    acc_sc[...] = jnp.zeros_like(acc_sc)
    for kb in range(nb):
        @pl.when(kb <= q_blk)
        def _(kb=kb):
            kk = k_ref[kb * MB_BLOCK:(kb + 1) * MB_BLOCK, :].astype(bf16)
            vv = v_ref[kb * MB_BLOCK:(kb + 1) * MB_BLOCK, :].astype(bf16)
            kpos = kb * MB_BLOCK + lax.broadcasted_iota(jnp.int32, (rows, MB_BLOCK), 1)
            s = _dot_t(q, kk) - slope * (qpos - kpos).astype(f32)
            chosen = jnp.sum(jnp.where(blk == kb, sel_f, 0.0), axis=1, keepdims=True)
            causal = jnp.where(kpos <= qpos, 1.0, 0.0)
            mask = jnp.where(kb == q_blk, causal, chosen) > 0.5
            m, l, acc = _softmax_step(s, vv, m_sc[...], l_sc[...], acc_sc[...], mask)
            m_sc[...] = m
            l_sc[...] = l
            acc_sc[...] = acc
    out = acc_sc[...] / l_sc[...]
    for h in range(H):
        o_ref[:, h * dh:(h + 1) * dh] = out[h * tq:(h + 1) * tq, :]


def moba_prompt(q, k, v, slopes, B, T):
    H, _, dh = q.shape
    tq = _pick(T, (128, 64, 32, 16))
    assert MB_BLOCK % tq == 0 and T % MB_BLOCK == 0
    nq, nb = T // tq, T // MB_BLOCK
    slope_rows = jnp.repeat(slopes.astype(f32), tq).reshape(H * tq, 1)
    return pl.pallas_call(
        functools.partial(_moba_prompt_kernel, H=H, tq=tq, nb=nb),
        grid=(B, nq),
        in_specs=[pl.BlockSpec((H, tq, dh), lambda b, qi: (0, b * nq + qi, 0)),
                  pl.BlockSpec((T, dh), lambda b, qi: (b, 0)), pl.BlockSpec((T, dh), lambda b, qi: (b, 0)),
                  pl.BlockSpec((H * tq, 1), lambda b, qi: (0, 0))],
        out_specs=pl.BlockSpec((tq, H * dh), lambda b, qi: (b * nq + qi, 0)),
        out_shape=jax.ShapeDtypeStruct((B * T, H * dh), f32),
        scratch_shapes=[pltpu.VMEM((H * tq, 1), f32), pltpu.VMEM((H * tq, 1), f32), pltpu.VMEM((H * tq, dh), f32)],
        compiler_params=_cp("parallel", "arbitrary"),
    )(q, k, v, slope_rows)


def _moba_sample_kernel(pt_ref, q_ref, kn_ref, vn_ref, slope_ref, kc_hbm, vc_hbm, o_ref, kbuf, vbuf, ksem, vsem,
                        *, layer, n_pages, page, DT):
    b = pl.program_id(0)
    nbatch = pl.num_programs(0)
    slot = b % 2
    pairs = [(kc_hbm, kbuf, ksem), (vc_hbm, vbuf, vsem)]

    @pl.when(b == 0)
    def _():
        _paged_start(pt_ref, 0, 0, n_pages, page, layer, pairs, 0)

    @pl.when(b + 1 < nbatch)
    def _():
        _paged_start(pt_ref, b + 1, 0, n_pages, page, layer, pairs, 1 - slot)

    _paged_wait(n_pages, page, layer, pairs, slot)

    q = q_ref[0]
    rows, dh = q.shape
    P = n_pages * page
    nb = P // MB_BLOCK
    sel = _moba_select(q, _block_means(kbuf[slot], nb), nb)
    sel_f = jnp.where(sel, 1.0, 0.0)
    blk = lax.broadcasted_iota(jnp.int32, sel.shape, 1)
    slope = slope_ref[...]
    t_idx = lax.broadcasted_iota(jnp.int32, (rows, MB_BLOCK), 0) % DT
    lane = lax.broadcasted_iota(jnp.int32, (rows, MB_BLOCK), 1)

    def body(kb, mla):
        off = pl.multiple_of(kb * MB_BLOCK, MB_BLOCK)
        kk = kbuf[slot, pl.ds(off, MB_BLOCK), :].astype(bf16)
        vv = vbuf[slot, pl.ds(off, MB_BLOCK), :].astype(bf16)
        dist = (P + t_idx) - (off + lane)
        s = _dot_t(q, kk) - slope * dist.astype(f32)
        chosen = jnp.sum(jnp.where(blk == kb, sel_f, 0.0), axis=1, keepdims=True) > 0.5
        return _softmax_step(s, vv, mla[0], mla[1], mla[2], jnp.broadcast_to(chosen, s.shape))

    m, l, acc = lax.fori_loop(0, nb, body, (jnp.full((rows, 1), NEG, f32), jnp.zeros((rows, 1), f32),
                                            jnp.zeros((rows, dh), f32)))
    t_new = lax.broadcasted_iota(jnp.int32, (rows, NEW_PAD), 0) % DT
    j_new = lax.broadcasted_iota(jnp.int32, (rows, NEW_PAD), 1)
    s = _dot_t(q, kn_ref[0].astype(bf16)) - slope * (t_new - j_new).astype(f32)
    m, l, acc = _softmax_step(s, vn_ref[0].astype(bf16), m, l, acc, j_new <= t_new)
    o_ref[0] = acc / l


def moba_sample(q, k_new, v_new, slopes, cache_k, cache_v, page_table, layer, DT):
    DB, rows, dh = q.shape
    H = rows // DT
    n_pages = page_table.shape[1]
    page = cache_k.shape[2]
    P = n_pages * page
    assert P % MB_BLOCK == 0 and DT < MB_BLOCK and P // MB_BLOCK <= LANES
    slope_rows = jnp.repeat(slopes.astype(f32), DT).reshape(H * DT, 1)
    scratch = [pltpu.VMEM((2, P, dh), f32), pltpu.VMEM((2, P, dh), f32),
               pltpu.SemaphoreType.DMA((2,)), pltpu.SemaphoreType.DMA((2,))]
    nd_map = lambda b, pt_: (0, 0)
    in_specs = [pl.BlockSpec((1, rows, dh), lambda b, pt_: (b, 0, 0)),
                pl.BlockSpec((1, NEW_PAD, dh), lambda b, pt_: (b, 0, 0)),
                pl.BlockSpec((1, NEW_PAD, dh), lambda b, pt_: (b, 0, 0)),
                pl.BlockSpec((rows, 1), nd_map),
                pl.BlockSpec(memory_space=pl.ANY), pl.BlockSpec(memory_space=pl.ANY)]
    return pl.pallas_call(
        functools.partial(_moba_sample_kernel, layer=layer, n_pages=n_pages, page=page, DT=DT),
        grid_spec=pltpu.PrefetchScalarGridSpec(
            num_scalar_prefetch=1, grid=(DB,), in_specs=in_specs,
            out_specs=pl.BlockSpec((1, rows, dh), lambda b, pt_: (b, 0, 0)), scratch_shapes=scratch),
        out_shape=jax.ShapeDtypeStruct((DB, rows, dh), f32),
        compiler_params=_cp("arbitrary"),
    )(page_table, q, k_new, v_new, slope_rows, cache_k, cache_v)


def _rope_tables(pos, rope):
    inv = ROPE_THETA ** (-jnp.arange(0, rope, 2, dtype=f32) / rope)
    ang = pos.astype(f32)[:, None] * inv[None, :]
    cos, sin = jnp.cos(ang), jnp.sin(ang)
    return jnp.concatenate([cos, cos], axis=-1), jnp.concatenate([-sin, sin], axis=-1)


def kernel(x_prompt, x_sample, cache_sb_k, cache_sb_v, cache_mla_ckv, cache_mla_kpe, cache_moba_k, cache_moba_v,
           page_table, ln_mix, w_in, q_a_norm, w_q_b, kv_a_norm, w_kv_b, g_mix, w_out, ln_ff,
           w_ff_gate, w_ff_up, w_ff_down, w_router, w_ex_gate, w_ex_up, w_ex_down, ln_final):
    B, T, D = x_prompt.shape
    DB, DT, _ = x_sample.shape
    depth = w_in.shape[0]
    in_width = w_in.shape[2]
    H = w_q_b.shape[2]
    sb_dh = cache_sb_k.shape[3]
    mb_dh = cache_moba_k.shape[3]
    kvl = cache_mla_ckv.shape[3]
    rope = cache_mla_kpe.shape[3]
    ql = q_a_norm.shape[1]
    nope = w_q_b.shape[3] - rope
    vdim = w_kv_b.shape[3] - nope
    dims = dict(H=H, sb_dh=sb_dh, mb_dh=mb_dh, ql=ql, kvl=kvl, rope=rope, nope=nope)
    assert in_width == H * sb_dh + 2 * sb_dh + ql + kvl + rope + H * mb_dh + 2 * mb_dh
    n_pages = page_table.shape[1]
    page = cache_sb_k.shape[2]
    P = n_pages * page
    Mp, Ms = B * T, DB * DT
    mla_scale = (nope + rope) ** -0.5
    slopes = 2.0 ** (-8.0 * jnp.arange(1, H + 1, dtype=f32) / H)

    pos = jnp.concatenate([jnp.tile(jnp.arange(T, dtype=jnp.int32), B),
                           jnp.tile(P + jnp.arange(DT, dtype=jnp.int32), DB)])
    cos64, sin64 = _rope_tables(pos, rope)
    in_pad = (-in_width) % 512

    def to_seq(a):
        w = a.shape[-1]
        return a.reshape(H, DB, DT, w).transpose(1, 0, 2, 3).reshape(DB, H * DT, w)

    def from_seq_heads(a):
        w = a.shape[-1]
        return a.reshape(DB, H, DT, w).transpose(1, 0, 2, 3).reshape(H, Ms, w)

    def from_seq_rows(a):
        w = a.shape[-1]
        return a.reshape(DB, H, DT, w).transpose(0, 2, 1, 3).reshape(Ms, H * w)

    def new_pad(a):
        return jnp.pad(a.reshape(DB, DT, a.shape[-1]), ((0, 0), (0, NEW_PAD - DT), (0, 0)))

    x = jnp.concatenate([x_prompt.reshape(Mp, D), x_sample.reshape(Ms, D)], axis=0)
    rows_p = [[] for _ in range(6)]
    rows_s = [[] for _ in range(6)]
    for l in range(depth):
        w_in_l = jnp.pad(w_in[l], ((0, 0), (0, in_pad))).astype(bf16)
        w_qn = w_q_b[l][:, :, :nope].reshape(ql, H * nope).astype(bf16)
        w_qr = w_q_b[l][:, :, nope:].reshape(ql, H * rope).astype(bf16)
        w_ukt = jnp.transpose(w_kv_b[l][:, :, :nope], (1, 2, 0)).astype(bf16)
        w_uv = jnp.transpose(w_kv_b[l][:, :, nope:], (1, 0, 2)).astype(bf16)

        h = rmsnorm(x, ln_mix[l], bf16)
        p = matmul(h, w_in_l)
        (sbq, mbq, qlat, qpe, sb_k, sb_v, ckv, kpe, mb_k, mb_v) = postproj(
            p, cos64, sin64, q_a_norm[l], kv_a_norm[l], w_qn, w_qr, w_ukt, dims)

        o_sb_p = sb_prompt(sbq, sb_k, sb_v, B, T)
        o_lat_p = mla_prompt(qlat, qpe, ckv, kpe, B, T, mla_scale)
        o_mb_p = moba_prompt(mbq, mb_k, mb_v, slopes, B, T)

        o_sb_s = sb_sample(to_seq(sbq[:, Mp:]), new_pad(sb_k[Mp:]), new_pad(sb_v[Mp:]),
                           cache_sb_k, cache_sb_v, page_table, l, DT)
        o_lat_s = mla_sample(to_seq(qlat[:, Mp:]), to_seq(qpe[:, Mp:]), new_pad(ckv[Mp:]), new_pad(kpe[Mp:]),
                             cache_mla_ckv, cache_mla_kpe, page_table, l, DT, mla_scale)
        o_mb_s = moba_sample(to_seq(mbq[:, Mp:]), new_pad(mb_k[Mp:]), new_pad(mb_v[Mp:]), slopes,
                             cache_moba_k, cache_moba_v, page_table, l, DT)

        o_sb = jnp.concatenate([o_sb_p, from_seq_rows(o_sb_s)], axis=0)
        o_lat = jnp.concatenate([o_lat_p, from_seq_heads(o_lat_s)], axis=1)
        o_mb = jnp.concatenate([o_mb_p, from_seq_rows(o_mb_s)], axis=0)
        cat = mixnorm(o_sb, o_lat, o_mb, w_uv, g_mix[l])
        x = matmul(cat, w_out[l].astype(bf16), residual=x)

        i = l // 2
        if l % 2 == 0:
            h2 = rmsnorm(x, ln_ff[l], bf16)
            a = gateup(h2, w_ff_gate[i][None].astype(bf16), w_ff_up[i][None].astype(bf16))
            x = matmul(a, w_ff_down[i].astype(bf16), residual=x)
        else:
            h2, gates = router(x, ln_ff[l], w_router[i])
            n_exp = w_router.shape[2]
            row_scale = gates[:, :n_exp].T.reshape(n_exp, Mp + Ms, 1)
            a = gateup(h2, w_ex_gate[i].astype(bf16), w_ex_up[i].astype(bf16), row_scale)
            d_exp = w_ex_down.shape[2]
            x = matmul(a, w_ex_down[i].reshape(n_exp * d_exp, D).astype(bf16), residual=x)

        for idx, arr in enumerate((sb_k, sb_v, ckv, kpe, mb_k, mb_v)):
            rows_p[idx].append(arr[:Mp].reshape(B, T, arr.shape[-1]))
            rows_s[idx].append(arr[Mp:].reshape(DB, DT, arr.shape[-1]))

    y = rmsnorm(x, ln_final, f32)
    y_prompt = y[:Mp].reshape(B, T, D)
    y_sample = y[Mp:].reshape(DB, DT, D)
    return (y_prompt, y_sample, *[jnp.stack(r, 0) for r in rows_p], *[jnp.stack(r, 0) for r in rows_s])
```

```python
import functools
import math

import jax
import jax.numpy as jnp
from jax import lax
from jax.experimental import pallas as pl
from jax.experimental.pallas import tpu as pltpu

MB_BLOCK = 256
MB_TOPK = 3
TOP_K = 2
ROPE_THETA = 10000.0
NORM_EPS = 1e-6

V7X_VMEM_LIMIT_BYTES = 56 * 1024 * 1024
LANES = 128
NEG = -1e30
NEW_PAD = 128

bf16 = jnp.bfloat16
f32 = jnp.float32


def _cp(*sem):
    return pltpu.CompilerParams(dimension_semantics=sem, vmem_limit_bytes=V7X_VMEM_LIMIT_BYTES)


def _pick(n, prefs):
    for p in prefs:
        if n % p == 0:
            return p
    return n


def _pick_k(k):
    if k <= 4096:
        return k
    for d in range(2, 65):
        if k % d == 0 and (k // d) % LANES == 0 and k // d <= 6144:
            return k // d
    return k


def _unroll(n):
    return _pick(n, (4, 2, 1))


def _dot(a, b):
    return jnp.dot(a, b, preferred_element_type=f32)


def _dot_t(a, b):
    return lax.dot_general(a, b, (((1,), (1,)), ((), ())), preferred_element_type=f32)


def _split(x):
    hi = x.astype(bf16)
    lo = (x - hi.astype(f32)).astype(bf16)
    return hi, lo


def _rms(x, g):
    return x * lax.rsqrt(jnp.mean(x * x, axis=-1, keepdims=True) + NORM_EPS) * g


def _rmsnorm_kernel(x_ref, g_ref, o_ref):
    o_ref[...] = _rms(x_ref[...].astype(f32), g_ref[...]).astype(o_ref.dtype)


def rmsnorm(x, g, out_dtype):
    m, d = x.shape
    tm = _pick(m, (512, 256, 128, 64, 32, 16))
    return pl.pallas_call(
        _rmsnorm_kernel,
        grid=(m // tm,),
        in_specs=[pl.BlockSpec((tm, d), lambda i: (i, 0)), pl.BlockSpec((1, d), lambda i: (0, 0))],
        out_specs=pl.BlockSpec((tm, d), lambda i: (i, 0)),
        out_shape=jax.ShapeDtypeStruct((m, d), out_dtype),
        compiler_params=_cp("parallel"),
    )(x, g.reshape(1, d).astype(f32))


def _mm_kernel(*refs, nk, has_res):
    if has_res:
        a_ref, w_ref, r_ref, o_ref = refs[:4]
    else:
        a_ref, w_ref, o_ref = refs[:3]
        r_ref = None
    part = _dot(a_ref[...], w_ref[...])

    def finish(v):
        if r_ref is not None:
            v = v + r_ref[...]
        o_ref[...] = v.astype(o_ref.dtype)

    if nk == 1:
        finish(part)
        return
    acc = refs[-1]
    k = pl.program_id(2)

    @pl.when(k == 0)
    def _():
        acc[...] = part

    @pl.when(k > 0)
    def _():
        acc[...] += part

    @pl.when(k == nk - 1)
    def _():
        finish(acc[...])


def matmul(a, w, residual=None, out_dtype=f32):
    m, kdim = a.shape
    n = w.shape[1]
    tm = _pick(m, (512, 256, 128, 64, 32, 16))
    tn = _pick(n, (512, 256, 128))
    tk = _pick_k(kdim)
    nk = kdim // tk
    in_specs = [pl.BlockSpec((tm, tk), lambda i, j, k: (i, k)), pl.BlockSpec((tk, tn), lambda i, j, k: (k, j))]
    args = [a, w]
    if residual is not None:
        in_specs.append(pl.BlockSpec((tm, tn), lambda i, j, k: (i, j)))
        args.append(residual)
    scratch = [pltpu.VMEM((tm, tn), f32)] if nk > 1 else []
    return pl.pallas_call(
        functools.partial(_mm_kernel, nk=nk, has_res=residual is not None),
        grid=(m // tm, n // tn, nk),
        in_specs=in_specs,
        out_specs=pl.BlockSpec((tm, tn), lambda i, j, k: (i, j)),
        out_shape=jax.ShapeDtypeStruct((m, n), out_dtype),
        scratch_shapes=scratch,
        compiler_params=_cp("parallel", "parallel", "arbitrary"),
    )(*args)


def _swiglu(g, u):
    return g / (1.0 + jnp.exp(-g)) * u


def _gateup_kernel(h_ref, wg_ref, wu_ref, o_ref):
    h = h_ref[...]
    o_ref[...] = _swiglu(_dot(h, wg_ref[...]), _dot(h, wu_ref[...])).astype(o_ref.dtype)


def gateup(h, wg, wu):
    m, d = h.shape
    f = wg.shape[1]
    tm = _pick(m, (512, 256, 128, 64, 32, 16))
    tn = _pick(f, (512, 256, 128))
    w_spec = pl.BlockSpec((d, tn), lambda j, i: (0, j))
    return pl.pallas_call(
        _gateup_kernel,
        grid=(f // tn, m // tm),
        in_specs=[pl.BlockSpec((tm, d), lambda j, i: (i, 0)), w_spec, w_spec],
        out_specs=pl.BlockSpec((tm, tn), lambda j, i: (i, j)),
        out_shape=jax.ShapeDtypeStruct((m, f), bf16),
        compiler_params=_cp("parallel", "parallel"),
    )(h, wg, wu)


def _postproj_kernel(p_ref, cos_ref, sin_ref, qan_ref, kvan_ref, wqn_ref, wqr_ref, wukt_ref,
                     sbq_ref, mbq_ref, qlat_ref, qpe_ref, sbk_ref, sbv_ref, ckv_ref, kpe_ref, mbk_ref, mbv_ref,
                     *, dims):
    H, sb_dh, mb_dh, ql, kvl, rope, nope = (dims[k] for k in ("H", "sb_dh", "mb_dh", "ql", "kvl", "rope", "nope"))
    sb_q0 = 0
    sb_k0 = sb_q0 + H * sb_dh
    sb_v0 = sb_k0 + sb_dh
    qa0 = sb_v0 + sb_dh
    kva0 = qa0 + ql
    kpe0 = kva0 + kvl
    mb_q0 = kpe0 + rope
    mb_k0 = mb_q0 + H * mb_dh
    mb_v0 = mb_k0 + mb_dh
    half = rope // 2
    cos = cos_ref[...]
    sin = sin_ref[...]

    def rot(x):
        return x * cos + jnp.concatenate([x[:, half:], x[:, :half]], axis=-1) * sin

    sb_scale = sb_dh ** -0.5
    mb_scale = mb_dh ** -0.5
    for h in range(H):
        sbq_ref[h] = (p_ref[:, sb_q0 + h * sb_dh:sb_q0 + (h + 1) * sb_dh] * sb_scale).astype(bf16)
        mbq_ref[h] = (p_ref[:, mb_q0 + h * mb_dh:mb_q0 + (h + 1) * mb_dh] * mb_scale).astype(bf16)
    sbk_ref[...] = p_ref[:, sb_k0:sb_k0 + sb_dh]
    sbv_ref[...] = p_ref[:, sb_v0:sb_v0 + sb_dh]
    mbk_ref[...] = p_ref[:, mb_k0:mb_k0 + mb_dh]
    mbv_ref[...] = p_ref[:, mb_v0:mb_v0 + mb_dh]
    ckv_ref[...] = _rms(p_ref[:, kva0:kva0 + kvl], kvan_ref[...])
    kpe_ref[...] = rot(p_ref[:, kpe0:kpe0 + rope])
    cq = _rms(p_ref[:, qa0:qa0 + ql], qan_ref[...]).astype(bf16)
    qn = _dot(cq, wqn_ref[...])
    qr = _dot(cq, wqr_ref[...])
    for h in range(H):
        qlat_ref[h] = _dot(qn[:, h * nope:(h + 1) * nope].astype(bf16), wukt_ref[h]).astype(bf16)
        qpe_ref[h] = rot(qr[:, h * rope:(h + 1) * rope]).astype(bf16)


def postproj(p, cos64, sin64, q_a_norm, kv_a_norm, w_qn, w_qr, w_ukt, dims):
    m, npad = p.shape
    H, sb_dh, mb_dh, ql, kvl, rope, nope = (dims[k] for k in ("H", "sb_dh", "mb_dh", "ql", "kvl", "rope", "nope"))
    tm = _pick(m, (256, 128, 64, 32, 16))
    row = lambda w: pl.BlockSpec((tm, w), lambda i: (i, 0))
    full = lambda a: pl.BlockSpec(a.shape, lambda i: (0,) * a.ndim)
    hd = lambda w: pl.BlockSpec((H, tm, w), lambda i: (0, i, 0))
    qan = q_a_norm.reshape(1, ql).astype(f32)
    kvan = kv_a_norm.reshape(1, kvl).astype(f32)
    out_shape = (
        jax.ShapeDtypeStruct((H, m, sb_dh), bf16), jax.ShapeDtypeStruct((H, m, mb_dh), bf16),
        jax.ShapeDtypeStruct((H, m, kvl), bf16), jax.ShapeDtypeStruct((H, m, rope), bf16),
        jax.ShapeDtypeStruct((m, sb_dh), f32), jax.ShapeDtypeStruct((m, sb_dh), f32),
        jax.ShapeDtypeStruct((m, kvl), f32), jax.ShapeDtypeStruct((m, rope), f32),
        jax.ShapeDtypeStruct((m, mb_dh), f32), jax.ShapeDtypeStruct((m, mb_dh), f32),
    )
    out_specs = (hd(sb_dh), hd(mb_dh), hd(kvl), hd(rope), row(sb_dh), row(sb_dh), row(kvl), row(rope), row(mb_dh), row(mb_dh))
    return pl.pallas_call(
        functools.partial(_postproj_kernel, dims=dims),
        grid=(m // tm,),
        in_specs=[row(npad), row(rope), row(rope), full(qan), full(kvan), full(w_qn), full(w_qr), full(w_ukt)],
        out_specs=out_specs,
        out_shape=out_shape,
        compiler_params=_cp("parallel"),
    )(p, cos64, sin64, qan, kvan, w_qn, w_qr, w_ukt)


def _mixnorm_kernel(osb_ref, olat_ref, omb_ref, wuv_ref, g_ref, o_ref, omla_sc, *, H, vdim):
    sbw = osb_ref.shape[1]
    mbw = omb_ref.shape[1]
    mlaw = H * vdim
    for h in range(H):
        omla_sc[:, h * vdim:(h + 1) * vdim] = _dot(olat_ref[h], wuv_ref[h])
    o_ref[:, :sbw] = _rms(osb_ref[...], g_ref[:, :sbw]).astype(bf16)
    o_ref[:, sbw:sbw + mlaw] = _rms(omla_sc[...], g_ref[:, sbw:sbw + mlaw]).astype(bf16)
    o_ref[:, sbw + mlaw:] = _rms(omb_ref[...], g_ref[:, sbw + mlaw:sbw + mlaw + mbw]).astype(bf16)


def mixnorm(o_sb, o_lat, o_mb, w_uv, g_mix):
    m, sbw = o_sb.shape
    H, _, kvl = o_lat.shape
    vdim = w_uv.shape[2]
    mbw = o_mb.shape[1]
    mix = sbw + H * vdim + mbw
    tm = _pick(m, (256, 128, 64, 32, 16))
    return pl.pallas_call(
        functools.partial(_mixnorm_kernel, H=H, vdim=vdim),
        grid=(m // tm,),
        in_specs=[pl.BlockSpec((tm, sbw), lambda i: (i, 0)), pl.BlockSpec((H, tm, kvl), lambda i: (0, i, 0)),
                  pl.BlockSpec((tm, mbw), lambda i: (i, 0)), pl.BlockSpec(w_uv.shape, lambda i: (0, 0, 0)),
                  pl.BlockSpec((1, mix), lambda i: (0, 0))],
        out_specs=pl.BlockSpec((tm, mix), lambda i: (i, 0)),
        out_shape=jax.ShapeDtypeStruct((m, mix), bf16),
        scratch_shapes=[pltpu.VMEM((tm, H * vdim), f32)],
        compiler_params=_cp("parallel"),
    )(o_sb, o_lat, o_mb, w_uv, g_mix.reshape(1, mix).astype(f32))


def _router_kernel(x_ref, g_ref, wr_ref, h_ref, gates_ref, *, n_exp):
    h = _rms(x_ref[...], g_ref[...])
    h_ref[...] = h
    h_hi, h_lo = _split(h)
    w_hi, w_lo = _split(wr_ref[...])
    logits = _dot(h_hi, w_hi) + (_dot(h_hi, w_lo) + _dot(h_lo, w_hi))
    lane = lax.broadcasted_iota(jnp.int32, logits.shape, 1)
    work = jnp.where(lane < n_exp, logits, -jnp.inf)
    vals, idxs = [], []
    for _ in range(TOP_K):
        mx = jnp.max(work, axis=1, keepdims=True)
        ix = jnp.min(jnp.where(work == mx, lane, LANES), axis=1, keepdims=True)
        vals.append(mx)
        idxs.append(ix)
        work = jnp.where(lane == ix, -jnp.inf, work)
    ex = [jnp.exp(v - vals[0]) for v in vals]
    den = ex[0]
    for e_ in ex[1:]:
        den = den + e_
    info = jnp.zeros_like(logits)
    for k_, (e_, ix) in enumerate(zip(ex, idxs)):
        info = jnp.where(lane == k_, ix.astype(f32), info)
        info = jnp.where(lane == TOP_K + k_, e_ / den, info)
    gates_ref[...] = info


def router(x, g, w_router):
    m, d = x.shape
    n_exp = w_router.shape[1]
    wr = jnp.pad(w_router.astype(f32), ((0, 0), (0, LANES - n_exp)))
    tm = _pick(m, (256, 128, 64, 32, 16))
    return pl.pallas_call(
        functools.partial(_router_kernel, n_exp=n_exp),
        grid=(m // tm,),
        in_specs=[pl.BlockSpec((tm, d), lambda i: (i, 0)), pl.BlockSpec((1, d), lambda i: (0, 0)),
                  pl.BlockSpec((d, LANES), lambda i: (0, 0))],
        out_specs=(pl.BlockSpec((tm, d), lambda i: (i, 0)), pl.BlockSpec((tm, LANES), lambda i: (i, 0))),
        out_shape=(jax.ShapeDtypeStruct((m, d), f32), jax.ShapeDtypeStruct((m, LANES), f32)),
        compiler_params=_cp("parallel"),
    )(x, g.reshape(1, d).astype(f32), wr)


MOE_ROW_TILE = 256


def _dispatch_plan(expert_ids, n_exp):
    m, k = expert_ids.shape
    tile = MOE_ROW_TILE
    n_rows = -(-(m * k) // tile) * tile + n_exp * tile
    flat = expert_ids.reshape(m * k)
    onehot = (flat[:, None] == jnp.arange(n_exp, dtype=jnp.int32)[None, :]).astype(jnp.int32)
    rank = jnp.sum((jnp.cumsum(onehot, axis=0) - onehot) * onehot, axis=1)
    padded = -(-jnp.sum(onehot, axis=0) // tile) * tile
    ends = jnp.cumsum(padded)
    pos = (ends - padded)[flat] + rank
    n_tiles = n_rows // tile
    tile_start = jnp.arange(n_tiles, dtype=jnp.int32) * tile
    tile_expert = jnp.minimum(jnp.sum((tile_start[:, None] >= ends[None, :]).astype(jnp.int32), axis=1), n_exp - 1)
    n_used = (ends[-1] // tile).astype(jnp.int32).reshape(1)
    src_tok = jnp.zeros((n_rows,), jnp.int32).at[pos].set(jnp.arange(m * k, dtype=jnp.int32) // k)
    return pos.astype(jnp.int32), src_tok, tile_expert.astype(jnp.int32), n_used, n_rows


def _gather_rows_kernel(src_ref, nused_ref, h_hbm, o_ref, buf, sem, *, tile):
    r = pl.program_id(0)

    @pl.when(r < nused_ref[0])
    def _():
        def start(i, _):
            pltpu.make_async_copy(h_hbm.at[pl.ds(src_ref[r * tile + i], 1), :], buf.at[pl.ds(i, 1), :], sem.at[0]).start()
            return 0
        lax.fori_loop(0, tile, start, 0)

        def wait(i, _):
            pltpu.make_async_copy(h_hbm.at[pl.ds(0, 1), :], buf.at[pl.ds(i, 1), :], sem.at[0]).wait()
            return 0
        lax.fori_loop(0, tile, wait, 0)
        o_ref[...] = buf[...].astype(o_ref.dtype)

    @pl.when(r >= nused_ref[0])
    def _():
        o_ref[...] = jnp.zeros_like(o_ref)


def gather_rows(h, src_tok, n_used, n_rows):
    m, d = h.shape
    tile = MOE_ROW_TILE
    return pl.pallas_call(
        functools.partial(_gather_rows_kernel, tile=tile),
        grid_spec=pltpu.PrefetchScalarGridSpec(
            num_scalar_prefetch=2, grid=(n_rows // tile,),
            in_specs=[pl.BlockSpec(memory_space=pl.ANY)],
            out_specs=pl.BlockSpec((tile, d), lambda r, src, nu: (r, 0)),
            scratch_shapes=[pltpu.VMEM((tile, d), f32), pltpu.SemaphoreType.DMA((1,))]),
        out_shape=jax.ShapeDtypeStruct((n_rows, d), bf16),
        compiler_params=_cp("arbitrary"),
    )(src_tok, n_used, h)


def _moe_gateup_kernel(te_ref, nused_ref, h_ref, wg_ref, wu_ref, o_ref):
    @pl.when(pl.program_id(1) < nused_ref[0])
    def _():
        h = h_ref[...]
        o_ref[...] = _swiglu(_dot(h, wg_ref[...]), _dot(h, wu_ref[...])).astype(o_ref.dtype)

    @pl.when(pl.program_id(1) >= nused_ref[0])
    def _():
        o_ref[...] = jnp.zeros_like(o_ref)


def moe_gateup(hs, wg, wu, tile_expert, n_used):
    n_rows, d = hs.shape
    f = wg.shape[2]
    tile = MOE_ROW_TILE
    tn = _pick(f, (1024, 512, 256, 128))
    row = lambda j, r, te, nu: (jnp.minimum(r, nu[0] - 1), 0)
    wmap = lambda j, r, te, nu: (te[jnp.minimum(r, nu[0] - 1)], 0, j)
    return pl.pallas_call(
        _moe_gateup_kernel,
        grid_spec=pltpu.PrefetchScalarGridSpec(
            num_scalar_prefetch=2, grid=(f // tn, n_rows // tile),
            in_specs=[pl.BlockSpec((tile, d), row), pl.BlockSpec((None, d, tn), wmap), pl.BlockSpec((None, d, tn), wmap)],
            out_specs=pl.BlockSpec((tile, tn), lambda j, r, te, nu: (r, j))),
        out_shape=jax.ShapeDtypeStruct((n_rows, f), bf16),
        compiler_params=_cp("arbitrary", "arbitrary"),
    )(tile_expert, n_used, hs, wg, wu)


def _moe_down_kernel(te_ref, nused_ref, a_ref, wd_ref, o_ref):
    @pl.when(pl.program_id(1) < nused_ref[0])
    def _():
        o_ref[...] = _dot(a_ref[...], wd_ref[...])

    @pl.when(pl.program_id(1) >= nused_ref[0])
    def _():
        o_ref[...] = jnp.zeros_like(o_ref)


def moe_down(a, wd, tile_expert, n_used):
    n_rows, f = a.shape
    d = wd.shape[2]
    tile = MOE_ROW_TILE
    tn = _pick(d, (1024, 512, 256, 128))
    return pl.pallas_call(
        _moe_down_kernel,
        grid_spec=pltpu.PrefetchScalarGridSpec(
            num_scalar_prefetch=2, grid=(d // tn, n_rows // tile),
            in_specs=[pl.BlockSpec((tile, f), lambda j, r, te, nu: (jnp.minimum(r, nu[0] - 1), 0)),
                      pl.BlockSpec((None, f, tn), lambda j, r, te, nu: (te[jnp.minimum(r, nu[0] - 1)], 0, j))],
            out_specs=pl.BlockSpec((tile, tn), lambda j, r, te, nu: (r, j))),
        out_shape=jax.ShapeDtypeStruct((n_rows, d), f32),
        compiler_params=_cp("arbitrary", "arbitrary"),
    )(tile_expert, n_used, a, wd)


def _moe_combine_kernel(pos_ref, x_ref, w_ref, y_hbm, o_ref, buf, sem, *, tile, k):
    i0 = pl.program_id(0) * tile

    def start(i, _):
        for kk in range(k):
            pltpu.make_async_copy(y_hbm.at[pl.ds(pos_ref[(i0 + i) * k + kk], 1), :], buf.at[kk, pl.ds(i, 1), :],
                                  sem.at[kk]).start()
        return 0
    lax.fori_loop(0, tile, start, 0)

    def wait(i, _):
        for kk in range(k):
            pltpu.make_async_copy(y_hbm.at[pl.ds(0, 1), :], buf.at[kk, pl.ds(i, 1), :], sem.at[kk]).wait()
        return 0
    lax.fori_loop(0, tile, wait, 0)
    out = x_ref[...]
    for kk in range(k):
        out = out + w_ref[:, kk:kk + 1] * buf[kk]
    o_ref[...] = out


def moe_combine(x, y, pos, weights):
    m, d = x.shape
    k = weights.shape[1]
    tile = _pick(m, (128, 64, 32, 16, 8))
    return pl.pallas_call(
        functools.partial(_moe_combine_kernel, tile=tile, k=k),
        grid_spec=pltpu.PrefetchScalarGridSpec(
            num_scalar_prefetch=1, grid=(m // tile,),
            in_specs=[pl.BlockSpec((tile, d), lambda i, p: (i, 0)), pl.BlockSpec((tile, k), lambda i, p: (i, 0)),
                      pl.BlockSpec(memory_space=pl.ANY)],
            out_specs=pl.BlockSpec((tile, d), lambda i, p: (i, 0)),
            scratch_shapes=[pltpu.VMEM((k, tile, d), f32), pltpu.SemaphoreType.DMA((k,))]),
        out_shape=jax.ShapeDtypeStruct((m, d), f32),
        compiler_params=_cp("arbitrary"),
    )(pos, x, weights, y)


def _upper(tk):
    j = lax.broadcasted_iota(jnp.int32, (tk, tk), 0)
    s = lax.broadcasted_iota(jnp.int32, (tk, tk), 1)
    return jnp.where(j > s, 1.0, 0.0).astype(bf16)


def _sb_block(q, k, v, causal, carry, acc):
    rows = q.shape[0]
    tk = k.shape[0]
    z = _dot_t(q, k)
    sp = jnp.maximum(z, 0.0) + jnp.log1p(jnp.exp(-jnp.abs(z)))
    l1 = jnp.where(causal, -sp, 0.0)
    hi, lo = _split(l1)
    cum = _dot(jnp.concatenate([hi, lo], axis=0), _upper(tk))
    later = carry + (cum[:rows] + cum[rows:])
    w = jnp.where(causal, jnp.exp(z - sp + later), 0.0)
    acc = acc + _dot(w.astype(bf16), v)
    carry = carry + jnp.sum(l1, axis=1, keepdims=True)
    return carry, acc


def _sb_past_blocks(q, kt, vt, carry, acc, nblk, tk):
    rows = q.shape[0]
    z = _dot(q, kt)
    zs = jnp.concatenate([z[:, b * tk:(b + 1) * tk] for b in range(nblk)], axis=0)
    sp = jnp.maximum(zs, 0.0) + jnp.log1p(jnp.exp(-jnp.abs(zs)))
    l1 = -sp
    hi, lo = _split(l1)
    cum = _dot(jnp.concatenate([hi, lo], axis=0), _upper(tk))
    cum = cum[:nblk * rows] + cum[nblk * rows:]
    tot = jnp.sum(l1, axis=1, keepdims=True)
    carries = [None] * nblk
    for b in reversed(range(nblk)):
        carries[b] = carry
        carry = carry + tot[b * rows:(b + 1) * rows]
    w = jnp.exp(zs - sp + cum + jnp.concatenate(carries, axis=0))
    wl = jnp.concatenate([w[b * rows:(b + 1) * rows] for b in range(nblk)], axis=1).astype(bf16)
    return carry, acc + _dot_t(wl, vt)


def _sb_prompt_kernel(q_ref, k_ref, v_ref, o_ref, carry_sc, acc_sc, *, H, tq, tk):
    qi = pl.program_id(1)
    st = pl.program_id(2)
    dh = q_ref.shape[2]
    rows = H * tq
    kb = (qi * tq + tq - 1) // tk - st

    @pl.when(st == 0)
    def _():
        carry_sc[...] = jnp.zeros_like(carry_sc)
        acc_sc[...] = jnp.zeros_like(acc_sc)

    @pl.when(kb >= 0)
    def _():
        q = q_ref[...].reshape(rows, dh)
        qpos = qi * tq + lax.broadcasted_iota(jnp.int32, (rows, tk), 0) % tq
        kpos = kb * tk + lax.broadcasted_iota(jnp.int32, (rows, tk), 1)
        carry, acc = _sb_block(q, k_ref[...].astype(bf16), v_ref[...].astype(bf16), kpos < qpos,
                               carry_sc[...], acc_sc[...])
        carry_sc[...] = carry
        acc_sc[...] = acc

    @pl.when(st == pl.num_programs(2) - 1)
    def _():
        for h in range(H):
            o_ref[:, h * dh:(h + 1) * dh] = acc_sc[h * tq:(h + 1) * tq, :]


def sb_prompt(q, k, v, B, T):
    H, _, dh = q.shape
    tq = _pick(T, (128, 64, 32, 16))
    tk = _pick(T, (256, 128, 64, 32, 16))
    nq, nk = T // tq, T // tk

    def kv_map(b, qi, st):
        return (b * nk + jnp.maximum((qi * tq + tq - 1) // tk - st, 0), 0)

    return pl.pallas_call(
        functools.partial(_sb_prompt_kernel, H=H, tq=tq, tk=tk),
        grid=(B, nq, nk),
        in_specs=[pl.BlockSpec((H, tq, dh), lambda b, qi, st: (0, b * nq + qi, 0)),
                  pl.BlockSpec((tk, dh), kv_map), pl.BlockSpec((tk, dh), kv_map)],
        out_specs=pl.BlockSpec((tq, H * dh), lambda b, qi, st: (b * nq + qi, 0)),
        out_shape=jax.ShapeDtypeStruct((B * T, H * dh), f32),
        scratch_shapes=[pltpu.VMEM((H * tq, 1), f32), pltpu.VMEM((H * tq, dh), f32)],
        compiler_params=_cp("parallel", "parallel", "arbitrary"),
    )(q, k, v)


def _page_dst(buf, slot, j, page, transposed):
    if transposed:
        return buf.at[slot, :, pl.ds(j * page, page)]
    return buf.at[slot, pl.ds(j * page, page), :]


def _paged_start(pt_ref, seq, first_page, n_pages, page, layer, pairs, slot):
    def body(j, _):
        pg = pt_ref[seq, first_page + j]
        for cache, buf, sem, tr in pairs:
            pltpu.make_async_copy(cache.at[layer, pg], _page_dst(buf, slot, j, page, tr), sem.at[slot]).start()
        return 0
    lax.fori_loop(0, n_pages, body, 0)


def _paged_wait(n_pages, page, layer, pairs, slot):
    def body(j, _):
        for cache, buf, sem, tr in pairs:
            pltpu.make_async_copy(cache.at[layer, 0], _page_dst(buf, slot, j, page, tr), sem.at[slot]).wait()
        return 0
    lax.fori_loop(0, n_pages, body, 0)


def _sb_sample_kernel(pt_ref, q_ref, kn_ref, vn_ref, kc_hbm, vc_hbm, o_ref, kbuf, vbuf, ksem, vsem,
                      *, layer, n_pages, page, DT, tk):
    b = pl.program_id(0)
    nb = pl.num_programs(0)
    slot = b % 2
    pairs = [(kc_hbm, kbuf, ksem, True), (vc_hbm, vbuf, vsem, True)]

    @pl.when(b == 0)
    def _():
        _paged_start(pt_ref, 0, 0, n_pages, page, layer, pairs, 0)

    @pl.when(b + 1 < nb)
    def _():
        _paged_start(pt_ref, b + 1, 0, n_pages, page, layer, pairs, 1 - slot)

    _paged_wait(n_pages, page, layer, pairs, slot)

    q = q_ref[0]
    rows, dh = q.shape
    t_idx = lax.broadcasted_iota(jnp.int32, (rows, NEW_PAD), 0) % DT
    j_idx = lax.broadcasted_iota(jnp.int32, (rows, NEW_PAD), 1)
    carry, acc = _sb_block(q, kn_ref[0].astype(bf16), vn_ref[0].astype(bf16), j_idx < t_idx,
                           jnp.zeros((rows, 1), f32), jnp.zeros((rows, dh), f32))
    n_grp = (n_pages * page) // tk
    per = _pick(n_grp, (16, 8, 4, 2, 1))
    for sb in reversed(range(n_grp // per)):
        lo, hi = sb * per * tk, (sb + 1) * per * tk
        carry, acc = _sb_past_blocks(q, kbuf[slot, :, lo:hi].astype(bf16), vbuf[slot, :, lo:hi].astype(bf16),
                                     carry, acc, per, tk)
    o_ref[0] = acc


def _sample_call(kernel, pt, blocked, caches, out_shape, out_block, scratch, grid, n_prefetch=1):
    nd = len(grid)
    def imap(shape):
        if nd == 1:
            return lambda b, pt_: (b,) + (0,) * (len(shape) - 1)
        return lambda b, c, pt_: (b,) + (0,) * (len(shape) - 1)
    in_specs = [pl.BlockSpec((1,) + a.shape[1:], imap(a.shape)) for a in blocked]
    in_specs += [pl.BlockSpec(memory_space=pl.ANY) for _ in caches]
    return pl.pallas_call(
        kernel,
        grid_spec=pltpu.PrefetchScalarGridSpec(
            num_scalar_prefetch=n_prefetch, grid=grid, in_specs=in_specs,
            out_specs=pl.BlockSpec(out_block, imap(out_shape.shape)), scratch_shapes=scratch),
        out_shape=out_shape,
        compiler_params=_cp(*(("arbitrary",) * nd)),
    )(pt, *blocked, *caches)


def sb_sample(q, k_new, v_new, cache_k, cache_v, page_table, layer, DT):
    DB, rows, dh = q.shape
    n_pages = page_table.shape[1]
    page = cache_k.shape[3]
    tk = 2 * page
    assert n_pages % 2 == 0
    P = n_pages * page
    scratch = [pltpu.VMEM((2, dh, P), f32), pltpu.VMEM((2, dh, P), f32),
               pltpu.SemaphoreType.DMA((2,)), pltpu.SemaphoreType.DMA((2,))]
    return _sample_call(
        functools.partial(_sb_sample_kernel, layer=layer, n_pages=n_pages, page=page, DT=DT, tk=tk),
        page_table, [q, k_new, v_new], [cache_k, cache_v],
        jax.ShapeDtypeStruct((DB, rows, dh), f32), (1, rows, dh), scratch, (DB,))


def _softmax_step(s, v, m_prev, l_prev, acc_prev, mask=None, v_transposed=False):
    if mask is not None:
        s = jnp.where(mask, s, NEG)
    m_new = jnp.maximum(m_prev, jnp.max(s, axis=1, keepdims=True))
    alpha = jnp.exp(m_prev - m_new)
    p = jnp.exp(s - m_new)
    if mask is not None:
        p = jnp.where(mask, p, 0.0)
    l_new = alpha * l_prev + jnp.sum(p, axis=1, keepdims=True)
    pb = p.astype(bf16)
    acc_new = alpha * acc_prev + (_dot_t(pb, v) if v_transposed else _dot(pb, v))
    return m_new, l_new, acc_new


def _mla_prompt_kernel(ql_ref, qp_ref, ckv_ref, kpe_ref, o_ref, m_sc, l_sc, acc_sc, *, H, tq, tk, scale):
    qi = pl.program_id(1)
    kj = pl.program_id(2)
    rows = H * tq
    last = (qi * tq + tq - 1) // tk

    @pl.when(kj == 0)
    def _():
        m_sc[...] = jnp.full_like(m_sc, NEG)
        l_sc[...] = jnp.zeros_like(l_sc)
        acc_sc[...] = jnp.zeros_like(acc_sc)

    @pl.when(kj <= last)
    def _():
        ql = ql_ref[...].reshape(rows, ql_ref.shape[2])
        qp = qp_ref[...].reshape(rows, qp_ref.shape[2])
        ck = ckv_ref[...].astype(bf16)
        kp = kpe_ref[...].astype(bf16)
        s = (_dot_t(ql, ck) + _dot_t(qp, kp)) * scale
        qpos = qi * tq + lax.broadcasted_iota(jnp.int32, (rows, tk), 0) % tq
        kpos = kj * tk + lax.broadcasted_iota(jnp.int32, (rows, tk), 1)
        m, l, acc = _softmax_step(s, ck, m_sc[...], l_sc[...], acc_sc[...], kpos <= qpos)
        m_sc[...] = m
        l_sc[...] = l
        acc_sc[...] = acc

    @pl.when(kj == pl.num_programs(2) - 1)
    def _():
        o_ref[...] = (acc_sc[...] / l_sc[...]).reshape(o_ref.shape).astype(o_ref.dtype)


def mla_prompt(q_lat, q_pe, ckv, kpe, B, T, scale):
    H, _, C = q_lat.shape
    R = q_pe.shape[2]
    tq = _pick(T, (128, 64, 32, 16))
    tk = _pick(T, (512, 256, 128, 64, 32, 16))
    nq, nk = T // tq, T // tk

    def kv_map(b, qi, kj):
        return (b * nk + jnp.minimum(kj, (qi * tq + tq - 1) // tk), 0)

    q_map = lambda b, qi, kj: (0, b * nq + qi, 0)
    return pl.pallas_call(
        functools.partial(_mla_prompt_kernel, H=H, tq=tq, tk=tk, scale=scale),
        grid=(B, nq, nk),
        in_specs=[pl.BlockSpec((H, tq, C), q_map), pl.BlockSpec((H, tq, R), q_map),
                  pl.BlockSpec((tk, C), kv_map), pl.BlockSpec((tk, R), kv_map)],
        out_specs=pl.BlockSpec((H, tq, C), q_map),
        out_shape=jax.ShapeDtypeStruct((H, B * T, C), bf16),
        scratch_shapes=[pltpu.VMEM((H * tq, 1), f32), pltpu.VMEM((H * tq, 1), f32), pltpu.VMEM((H * tq, C), f32)],
        compiler_params=_cp("parallel", "parallel", "arbitrary"),
    )(q_lat, q_pe, ckv, kpe)


def _mla_sample_kernel(pt_ref, ql_ref, qp_ref, cn_ref, pn_ref, ckv_hbm, kpe_hbm, o_ref,
                       cbuf, pbuf, csem, psem, m_sc, l_sc, acc_sc, *, layer, chunk, page, DT, scale):
    b = pl.program_id(0)
    c = pl.program_id(1)
    nc = pl.num_programs(1)
    step = b * nc + c
    total = pl.num_programs(0) * nc
    slot = step % 2
    pairs = [(ckv_hbm, cbuf, csem, False), (kpe_hbm, pbuf, psem, True)]

    @pl.when(step == 0)
    def _():
        _paged_start(pt_ref, 0, 0, chunk, page, layer, pairs, 0)

    @pl.when(step + 1 < total)
    def _():
        nxt = step + 1
        _paged_start(pt_ref, nxt // nc, (nxt % nc) * chunk, chunk, page, layer, pairs, 1 - slot)

    _paged_wait(chunk, page, layer, pairs, slot)

    ql = ql_ref[0]
    qp = qp_ref[0]
    rows = ql.shape[0]

    @pl.when(c == 0)
    def _():
        cn = cn_ref[0].astype(bf16)
        pn = pn_ref[0].astype(bf16)
        s = (_dot_t(ql, cn) + _dot_t(qp, pn)) * scale
        t_idx = lax.broadcasted_iota(jnp.int32, (rows, NEW_PAD), 0) % DT
        j_idx = lax.broadcasted_iota(jnp.int32, (rows, NEW_PAD), 1)
        m, l, acc = _softmax_step(s, cn, jnp.full((rows, 1), NEG, f32), jnp.zeros((rows, 1), f32),
                                  jnp.zeros(acc_sc.shape, f32), j_idx <= t_idx)
        m_sc[...] = m
        l_sc[...] = l
        acc_sc[...] = acc

    ck = cbuf[slot].astype(bf16)
    kpt = pbuf[slot].astype(bf16)
    s = (_dot_t(ql, ck) + _dot(qp, kpt)) * scale
    m, l, acc = _softmax_step(s, ck, m_sc[...], l_sc[...], acc_sc[...])
    m_sc[...] = m
    l_sc[...] = l
    acc_sc[...] = acc

    @pl.when(c == nc - 1)
    def _():
        o_ref[0] = (acc_sc[...] / l_sc[...]).astype(o_ref.dtype)


def mla_sample(q_lat, q_pe, ckv_new, kpe_new, cache_ckv, cache_kpe, page_table, layer, DT, scale):
    DB, rows, C = q_lat.shape
    R = q_pe.shape[2]
    n_pages = page_table.shape[1]
    page = cache_ckv.shape[2]
    chunk = _pick(n_pages, (32, 16, 8, 4, 2, 1))
    scratch = [pltpu.VMEM((2, chunk * page, C), f32), pltpu.VMEM((2, R, chunk * page), f32),
               pltpu.SemaphoreType.DMA((2,)), pltpu.SemaphoreType.DMA((2,)),
               pltpu.VMEM((rows, 1), f32), pltpu.VMEM((rows, 1), f32), pltpu.VMEM((rows, C), f32)]
    return _sample_call(
        functools.partial(_mla_sample_kernel, layer=layer, chunk=chunk, page=page, DT=DT, scale=scale),
        page_table, [q_lat, q_pe, ckv_new, kpe_new], [cache_ckv, cache_kpe],
        jax.ShapeDtypeStruct((DB, rows, C), bf16), (1, rows, C), scratch, (DB, n_pages // chunk))


def _moba_select(q, kmean, n_valid):
    nb, dh = kmean.shape
    km = jnp.concatenate([kmean, jnp.zeros((LANES - nb, dh), f32)], axis=0) if nb < LANES else kmean
    k_hi, k_lo = _split(km)
    return _moba_topk(_dot_t(q, k_hi) + _dot_t(q, k_lo), n_valid)


def _moba_topk(gate, n_valid):
    blk = lax.broadcasted_iota(jnp.int32, gate.shape, 1)
    valid = blk < n_valid
    work = jnp.where(valid, gate, -jnp.inf)
    sel = jnp.zeros(gate.shape, jnp.bool_)
    for _ in range(MB_TOPK):
        mx = jnp.max(work, axis=1, keepdims=True)
        ix = jnp.min(jnp.where(work == mx, blk, LANES), axis=1, keepdims=True)
        pick = blk == ix
        sel = jnp.logical_or(sel, pick)
        work = jnp.where(pick, -jnp.inf, work)
    return jnp.logical_and(sel, valid)


def _block_means(k, nb):
    return jnp.mean(k.reshape(nb, MB_BLOCK, k.shape[1]), axis=1)


def _moba_prompt_kernel(q_ref, k_ref, v_ref, slope_ref, o_ref, m_sc, l_sc, acc_sc, *, H, tq, nb):
    qi = pl.program_id(1)
    dh = q_ref.shape[2]
    rows = H * tq
    q = q_ref[...].reshape(rows, dh)
    q_blk = (qi * tq) // MB_BLOCK
    sel = _moba_select(q, _block_means(k_ref[...], nb), q_blk)
    sel_f = jnp.where(sel, 1.0, 0.0)
    blk = lax.broadcasted_iota(jnp.int32, sel.shape, 1)
    slope = slope_ref[...]
    qpos = qi * tq + lax.broadcasted_iota(jnp.int32, (rows, MB_BLOCK), 0) % tq
    m_sc[...] = jnp.full_like(m_sc, NEG)
    l_sc[...] = jnp.zeros_like(l_sc)
    acc_sc[...] = jnp.zeros_like(acc_sc)
    for kb in range(nb):
        @pl.when(kb <= q_blk)
        def _(kb=kb):
            kk = k_ref[kb * MB_BLOCK:(kb + 1) * MB_BLOCK, :].astype(bf16)
            vv = v_ref[kb * MB_BLOCK:(kb + 1) * MB_BLOCK, :].astype(bf16)
            kpos = kb * MB_BLOCK + lax.broadcasted_iota(jnp.int32, (rows, MB_BLOCK), 1)
            s = _dot_t(q, kk) - slope * (qpos - kpos).astype(f32)
            chosen = jnp.sum(jnp.where(blk == kb, sel_f, 0.0), axis=1, keepdims=True)
            causal = jnp.where(kpos <= qpos, 1.0, 0.0)
            mask = jnp.where(kb == q_blk, causal, chosen) > 0.5
            m, l, acc = _softmax_step(s, vv, m_sc[...], l_sc[...], acc_sc[...], mask)
            m_sc[...] = m
            l_sc[...] = l
            acc_sc[...] = acc
    out = acc_sc[...] / l_sc[...]
    for h in range(H):
        o_ref[:, h * dh:(h + 1) * dh] = out[h * tq:(h + 1) * tq, :]


def moba_prompt(q, k, v, slopes, B, T):
    H, _, dh = q.shape
    tq = _pick(T, (128, 64, 32, 16))
    assert MB_BLOCK % tq == 0 and T % MB_BLOCK == 0
    nq, nb = T // tq, T // MB_BLOCK
    slope_rows = jnp.repeat(slopes.astype(f32), tq).reshape(H * tq, 1)
    return pl.pallas_call(
        functools.partial(_moba_prompt_kernel, H=H, tq=tq, nb=nb),
        grid=(B, nq),
        in_specs=[pl.BlockSpec((H, tq, dh), lambda b, qi: (0, b * nq + qi, 0)),
                  pl.BlockSpec((T, dh), lambda b, qi: (b, 0)), pl.BlockSpec((T, dh), lambda b, qi: (b, 0)),
                  pl.BlockSpec((H * tq, 1), lambda b, qi: (0, 0))],
        out_specs=pl.BlockSpec((tq, H * dh), lambda b, qi: (b * nq + qi, 0)),
        out_shape=jax.ShapeDtypeStruct((B * T, H * dh), f32),
        scratch_shapes=[pltpu.VMEM((H * tq, 1), f32), pltpu.VMEM((H * tq, 1), f32), pltpu.VMEM((H * tq, dh), f32)],
        compiler_params=_cp("parallel", "arbitrary"),
    )(q, k, v, slope_rows)


def _moba_sample_kernel(pt_ref, q_ref, kn_ref, vn_ref, slope_ref, kc_hbm, vc_hbm, o_ref, kbuf, vbuf, ksem, vsem,
                        *, layer, n_pages, page, DT):
    b = pl.program_id(0)
    nbatch = pl.num_programs(0)
    slot = b % 2
    pairs = [(kc_hbm, kbuf, ksem, True), (vc_hbm, vbuf, vsem, True)]

    @pl.when(b == 0)
    def _():
        _paged_start(pt_ref, 0, 0, n_pages, page, layer, pairs, 0)

    @pl.when(b + 1 < nbatch)
    def _():
        _paged_start(pt_ref, b + 1, 0, n_pages, page, layer, pairs, 1 - slot)

    _paged_wait(n_pages, page, layer, pairs, slot)

    q = q_ref[0]
    rows, dh = q.shape
    P = n_pages * page
    nb = P // MB_BLOCK
    col_id = lax.broadcasted_iota(jnp.int32, (dh, LANES), 1)
    kmt = jnp.zeros((dh, LANES), f32)
    for kb in range(nb):
        col = jnp.sum(kbuf[slot, :, kb * MB_BLOCK:(kb + 1) * MB_BLOCK], axis=1, keepdims=True) * (1.0 / MB_BLOCK)
        kmt = jnp.where(col_id == kb, col, kmt)
    km_hi, km_lo = _split(kmt)
    sel = _moba_topk(_dot(q, km_hi) + _dot(q, km_lo), nb)
    sel_f = jnp.where(sel, 1.0, 0.0)
    blk = lax.broadcasted_iota(jnp.int32, sel.shape, 1)
    slope = slope_ref[...]
    per = _pick(nb, (16, 8, 4, 2, 1))
    width = per * MB_BLOCK
    t_idx = lax.broadcasted_iota(jnp.int32, (rows, width), 0) % DT
    lane = lax.broadcasted_iota(jnp.int32, (rows, width), 1)
    rel = (t_idx - lane).astype(f32)
    m = jnp.full((rows, 1), NEG, f32)
    l = jnp.zeros((rows, 1), f32)
    acc = jnp.zeros((rows, dh), f32)
    for sb in range(nb // per):
        lo = sb * width
        kt = kbuf[slot, :, lo:lo + width].astype(bf16)
        vt = vbuf[slot, :, lo:lo + width].astype(bf16)
        s = _dot(q, kt) - slope * (rel + float(P - lo))
        chosen = [jnp.sum(jnp.where(blk == sb * per + b, sel_f, 0.0), axis=1, keepdims=True) for b in range(per)]
        mask = jnp.concatenate([jnp.broadcast_to(c, (rows, MB_BLOCK)) for c in chosen], axis=1) > 0.5
        m, l, acc = _softmax_step(s, vt, m, l, acc, mask, v_transposed=True)
    t_new = lax.broadcasted_iota(jnp.int32, (rows, NEW_PAD), 0) % DT
    j_new = lax.broadcasted_iota(jnp.int32, (rows, NEW_PAD), 1)
    s = _dot_t(q, kn_ref[0].astype(bf16)) - slope * (t_new - j_new).astype(f32)
    m, l, acc = _softmax_step(s, vn_ref[0].astype(bf16), m, l, acc, j_new <= t_new)
    o_ref[0] = acc / l


def moba_sample(q, k_new, v_new, slopes, cache_k, cache_v, page_table, layer, DT):
    DB, rows, dh = q.shape
    H = rows // DT
    n_pages = page_table.shape[1]
    page = cache_k.shape[3]
    P = n_pages * page
    assert P % MB_BLOCK == 0 and DT < MB_BLOCK and P // MB_BLOCK <= LANES
    slope_rows = jnp.repeat(slopes.astype(f32), DT).reshape(H * DT, 1)
    scratch = [pltpu.VMEM((2, dh, P), f32), pltpu.VMEM((2, dh, P), f32),
               pltpu.SemaphoreType.DMA((2,)), pltpu.SemaphoreType.DMA((2,))]
    nd_map = lambda b, pt_: (0, 0)
    in_specs = [pl.BlockSpec((1, rows, dh), lambda b, pt_: (b, 0, 0)),
                pl.BlockSpec((1, NEW_PAD, dh), lambda b, pt_: (b, 0, 0)),
                pl.BlockSpec((1, NEW_PAD, dh), lambda b, pt_: (b, 0, 0)),
                pl.BlockSpec((rows, 1), nd_map),
                pl.BlockSpec(memory_space=pl.ANY), pl.BlockSpec(memory_space=pl.ANY)]
    return pl.pallas_call(
        functools.partial(_moba_sample_kernel, layer=layer, n_pages=n_pages, page=page, DT=DT),
        grid_spec=pltpu.PrefetchScalarGridSpec(
            num_scalar_prefetch=1, grid=(DB,), in_specs=in_specs,
            out_specs=pl.BlockSpec((1, rows, dh), lambda b, pt_: (b, 0, 0)), scratch_shapes=scratch),
        out_shape=jax.ShapeDtypeStruct((DB, rows, dh), f32),
        compiler_params=_cp("arbitrary"),
    )(page_table, q, k_new, v_new, slope_rows, cache_k, cache_v)


def _rope_tables(pos, rope):
    inv = ROPE_THETA ** (-jnp.arange(0, rope, 2, dtype=f32) / rope)
    ang = pos.astype(f32)[:, None] * inv[None, :]
    cos, sin = jnp.cos(ang), jnp.sin(ang)
    return jnp.concatenate([cos, cos], axis=-1), jnp.concatenate([-sin, sin], axis=-1)


def kernel(x_prompt, x_sample, cache_sb_k, cache_sb_v, cache_mla_ckv, cache_mla_kpe, cache_moba_k, cache_moba_v,
           page_table, ln_mix, w_in, q_a_norm, w_q_b, kv_a_norm, w_kv_b, g_mix, w_out, ln_ff,
           w_ff_gate, w_ff_up, w_ff_down, w_router, w_ex_gate, w_ex_up, w_ex_down, ln_final):
    B, T, D = x_prompt.shape
    DB, DT, _ = x_sample.shape
    depth = w_in.shape[0]
    in_width = w_in.shape[2]
    H = w_q_b.shape[2]
    sb_dh = cache_sb_k.shape[3]
    mb_dh = cache_moba_k.shape[3]
    kvl = cache_mla_ckv.shape[3]
    rope = cache_mla_kpe.shape[3]
    ql = q_a_norm.shape[1]
    nope = w_q_b.shape[3] - rope
    vdim = w_kv_b.shape[3] - nope
    dims = dict(H=H, sb_dh=sb_dh, mb_dh=mb_dh, ql=ql, kvl=kvl, rope=rope, nope=nope)
    assert in_width == H * sb_dh + 2 * sb_dh + ql + kvl + rope + H * mb_dh + 2 * mb_dh
    n_pages = page_table.shape[1]
    page = cache_sb_k.shape[2]
    P = n_pages * page
    Mp, Ms = B * T, DB * DT
    mla_scale = (nope + rope) ** -0.5
    slopes = 2.0 ** (-8.0 * jnp.arange(1, H + 1, dtype=f32) / H)

    pos = jnp.concatenate([jnp.tile(jnp.arange(T, dtype=jnp.int32), B),
                           jnp.tile(P + jnp.arange(DT, dtype=jnp.int32), DB)])
    cos64, sin64 = _rope_tables(pos, rope)
    in_pad = (-in_width) % 512

    def to_seq(a):
        w = a.shape[-1]
        return a.reshape(H, DB, DT, w).transpose(1, 0, 2, 3).reshape(DB, H * DT, w)

    def from_seq_heads(a):
        w = a.shape[-1]
        return a.reshape(DB, H, DT, w).transpose(1, 0, 2, 3).reshape(H, Ms, w)

    def from_seq_rows(a):
        w = a.shape[-1]
        return a.reshape(DB, H, DT, w).transpose(0, 2, 1, 3).reshape(Ms, H * w)

    def new_pad(a):
        return jnp.pad(a.reshape(DB, DT, a.shape[-1]), ((0, 0), (0, NEW_PAD - DT), (0, 0)))

    sbk_t, sbv_t, kpe_t, mbk_t, mbv_t = (jnp.swapaxes(c, 2, 3) for c in
                                         (cache_sb_k, cache_sb_v, cache_mla_kpe, cache_moba_k, cache_moba_v))

    x = jnp.concatenate([x_prompt.reshape(Mp, D), x_sample.reshape(Ms, D)], axis=0)
    rows_p = [[] for _ in range(6)]
    rows_s = [[] for _ in range(6)]
    for l in range(depth):
        w_in_l = jnp.pad(w_in[l], ((0, 0), (0, in_pad))).astype(bf16)
        w_qn = w_q_b[l][:, :, :nope].reshape(ql, H * nope).astype(bf16)
        w_qr = w_q_b[l][:, :, nope:].reshape(ql, H * rope).astype(bf16)
        w_ukt = jnp.transpose(w_kv_b[l][:, :, :nope], (1, 2, 0)).astype(bf16)
        w_uv = jnp.transpose(w_kv_b[l][:, :, nope:], (1, 0, 2)).astype(bf16)

        h = rmsnorm(x, ln_mix[l], bf16)
        p = matmul(h, w_in_l)
        (sbq, mbq, qlat, qpe, sb_k, sb_v, ckv, kpe, mb_k, mb_v) = postproj(
            p, cos64, sin64, q_a_norm[l], kv_a_norm[l], w_qn, w_qr, w_ukt, dims)

        o_sb_p = sb_prompt(sbq, sb_k, sb_v, B, T)
        o_lat_p = mla_prompt(qlat, qpe, ckv, kpe, B, T, mla_scale)
        o_mb_p = moba_prompt(mbq, mb_k, mb_v, slopes, B, T)

        o_sb_s = sb_sample(to_seq(sbq[:, Mp:]), new_pad(sb_k[Mp:]), new_pad(sb_v[Mp:]),
                           sbk_t, sbv_t, page_table, l, DT)
        o_lat_s = mla_sample(to_seq(qlat[:, Mp:]), to_seq(qpe[:, Mp:]), new_pad(ckv[Mp:]), new_pad(kpe[Mp:]),
                             cache_mla_ckv, kpe_t, page_table, l, DT, mla_scale)
        o_mb_s = moba_sample(to_seq(mbq[:, Mp:]), new_pad(mb_k[Mp:]), new_pad(mb_v[Mp:]), slopes,
                             mbk_t, mbv_t, page_table, l, DT)

        o_sb = jnp.concatenate([o_sb_p, from_seq_rows(o_sb_s)], axis=0)
        o_lat = jnp.concatenate([o_lat_p, from_seq_heads(o_lat_s)], axis=1)
        o_mb = jnp.concatenate([o_mb_p, from_seq_rows(o_mb_s)], axis=0)
        cat = mixnorm(o_sb, o_lat, o_mb, w_uv, g_mix[l])
        x = matmul(cat, w_out[l].astype(bf16), residual=x)

        i = l // 2
        if l % 2 == 0:
            h2 = rmsnorm(x, ln_ff[l], bf16)
            a = gateup(h2, w_ff_gate[i].astype(bf16), w_ff_up[i].astype(bf16))
            x = matmul(a, w_ff_down[i].astype(bf16), residual=x)
        else:
            h2, info = router(x, ln_ff[l], w_router[i])
            n_exp = w_router.shape[2]
            pos, src_tok, tile_expert, n_used, n_rows = _dispatch_plan(info[:, :TOP_K].astype(jnp.int32), n_exp)
            hs = gather_rows(h2, src_tok, n_used, n_rows)
            a = moe_gateup(hs, w_ex_gate[i].astype(bf16), w_ex_up[i].astype(bf16), tile_expert, n_used)
            y = moe_down(a, w_ex_down[i].astype(bf16), tile_expert, n_used)
            x = moe_combine(x, y, pos, info[:, TOP_K:2 * TOP_K])

        for idx, arr in enumerate((sb_k, sb_v, ckv, kpe, mb_k, mb_v)):
            rows_p[idx].append(arr[:Mp].reshape(B, T, arr.shape[-1]))
            rows_s[idx].append(arr[Mp:].reshape(DB, DT, arr.shape[-1]))

    y = rmsnorm(x, ln_final, f32)
    y_prompt = y[:Mp].reshape(B, T, D)
    y_sample = y[Mp:].reshape(DB, DT, D)
    return (y_prompt, y_sample, *[jnp.stack(r, 0) for r in rows_p], *[jnp.stack(r, 0) for r in rows_s])
```

```python
import functools
import math

import jax
import jax.numpy as jnp
from jax import lax
from jax.experimental import pallas as pl
from jax.experimental.pallas import tpu as pltpu

MB_BLOCK = 256
MB_TOPK = 3
TOP_K = 2
ROPE_THETA = 10000.0
NORM_EPS = 1e-6

V7X_VMEM_LIMIT_BYTES = 56 * 1024 * 1024
LANES = 128
NEG = -1e30
NEW_PAD = 128
HEAD_GROUP = 4
MLA_SAMPLE_PARTS = 2

bf16 = jnp.bfloat16
f32 = jnp.float32


def _cp(*sem):
    return pltpu.CompilerParams(dimension_semantics=sem, vmem_limit_bytes=V7X_VMEM_LIMIT_BYTES)


def _pick(n, prefs):
    for p in prefs:
        if n % p == 0:
            return p
    return n


def _pick_k(k):
    if k <= 4096:
        return k
    for d in range(2, 65):
        if k % d == 0 and (k // d) % LANES == 0 and k // d <= 6144:
            return k // d
    return k


def _dot(a, b):
    return jnp.dot(a, b, preferred_element_type=f32)


def _dot_t(a, b):
    return lax.dot_general(a, b, (((1,), (1,)), ((), ())), preferred_element_type=f32)


def _split(x):
    hi = x.astype(bf16)
    lo = (x - hi.astype(f32)).astype(bf16)
    return hi, lo


def _rms(x, g):
    return x * lax.rsqrt(jnp.mean(x * x, axis=-1, keepdims=True) + NORM_EPS) * g


def _rmsnorm_kernel(x_ref, g_ref, o_ref):
    o_ref[...] = _rms(x_ref[...].astype(f32), g_ref[...]).astype(o_ref.dtype)


def rmsnorm(x, g, out_dtype):
    m, d = x.shape
    tm = _pick(m, (512, 256, 128, 64, 32, 16))
    return pl.pallas_call(
        _rmsnorm_kernel,
        grid=(m // tm,),
        in_specs=[pl.BlockSpec((tm, d), lambda i: (i, 0)), pl.BlockSpec((1, d), lambda i: (0, 0))],
        out_specs=pl.BlockSpec((tm, d), lambda i: (i, 0)),
        out_shape=jax.ShapeDtypeStruct((m, d), out_dtype),
        compiler_params=_cp("parallel"),
    )(x, g.reshape(1, d).astype(f32))


def _mm_kernel(*refs, nk, has_res):
    if has_res:
        a_ref, w_ref, r_ref, o_ref = refs[:4]
    else:
        a_ref, w_ref, o_ref = refs[:3]
        r_ref = None
    part = _dot(a_ref[...], w_ref[...])

    def finish(v):
        if r_ref is not None:
            v = v + r_ref[...]
        o_ref[...] = v.astype(o_ref.dtype)

    if nk == 1:
        finish(part)
        return
    acc = refs[-1]
    k = pl.program_id(2)

    @pl.when(k == 0)
    def _():
        acc[...] = part

    @pl.when(k > 0)
    def _():
        acc[...] += part

    @pl.when(k == nk - 1)
    def _():
        finish(acc[...])


def matmul(a, w, residual=None, out_dtype=f32):
    m, kdim = a.shape
    n = w.shape[1]
    tm = _pick(m, (512, 256, 128, 64, 32, 16))
    tn = _pick(n, (512, 256, 128))
    tk = _pick_k(kdim)
    nk = kdim // tk
    in_specs = [pl.BlockSpec((tm, tk), lambda i, j, k: (i, k)), pl.BlockSpec((tk, tn), lambda i, j, k: (k, j))]
    args = [a, w]
    if residual is not None:
        in_specs.append(pl.BlockSpec((tm, tn), lambda i, j, k: (i, j)))
        args.append(residual)
    scratch = [pltpu.VMEM((tm, tn), f32)] if nk > 1 else []
    return pl.pallas_call(
        functools.partial(_mm_kernel, nk=nk, has_res=residual is not None),
        grid=(m // tm, n // tn, nk),
        in_specs=in_specs,
        out_specs=pl.BlockSpec((tm, tn), lambda i, j, k: (i, j)),
        out_shape=jax.ShapeDtypeStruct((m, n), out_dtype),
        scratch_shapes=scratch,
        compiler_params=_cp("parallel", "parallel", "arbitrary"),
    )(*args)


def _swiglu(g, u):
    return g / (1.0 + jnp.exp(-g)) * u


def _gateup_kernel(h_ref, wg_ref, wu_ref, o_ref, wg_sc, wu_sc):
    @pl.when(pl.program_id(1) == 0)
    def _():
        wg_sc[...] = wg_ref[...].astype(bf16)
        wu_sc[...] = wu_ref[...].astype(bf16)

    h = h_ref[...]
    o_ref[...] = _swiglu(_dot(h, wg_sc[...]), _dot(h, wu_sc[...])).astype(o_ref.dtype)


def gateup(h, wg, wu):
    m, d = h.shape
    f = wg.shape[1]
    tm = _pick(m, (512, 256, 128, 64, 32, 16))
    tn = _pick(f, (512, 256, 128))
    w_spec = pl.BlockSpec((d, tn), lambda j, i: (0, j))
    return pl.pallas_call(
        _gateup_kernel,
        grid=(f // tn, m // tm),
        in_specs=[pl.BlockSpec((tm, d), lambda j, i: (i, 0)), w_spec, w_spec],
        out_specs=pl.BlockSpec((tm, tn), lambda j, i: (i, j)),
        out_shape=jax.ShapeDtypeStruct((m, f), bf16),
        scratch_shapes=[pltpu.VMEM((d, tn), bf16), pltpu.VMEM((d, tn), bf16)],
        compiler_params=_cp("arbitrary", "arbitrary"),
    )(h, wg, wu)


def _postproj_kernel(p_ref, cos_ref, sin_ref, qan_ref, kvan_ref, wqn_ref, wqr_ref, wukt_ref,
                     sbq_ref, mbq_ref, qlat_ref, qpe_ref, sbk_ref, sbv_ref, ckv_ref, kpe_ref, mbk_ref, mbv_ref,
                     *, dims):
    H, sb_dh, mb_dh, ql, kvl, rope, nope = (dims[k] for k in ("H", "sb_dh", "mb_dh", "ql", "kvl", "rope", "nope"))
    sb_q0 = 0
    sb_k0 = sb_q0 + H * sb_dh
    sb_v0 = sb_k0 + sb_dh
    qa0 = sb_v0 + sb_dh
    kva0 = qa0 + ql
    kpe0 = kva0 + kvl
    mb_q0 = kpe0 + rope
    mb_k0 = mb_q0 + H * mb_dh
    mb_v0 = mb_k0 + mb_dh
    half = rope // 2
    cos = cos_ref[...]
    sin = sin_ref[...]

    def rot(x):
        return x * cos + jnp.concatenate([x[:, half:], x[:, :half]], axis=-1) * sin

    sb_scale = sb_dh ** -0.5
    mb_scale = mb_dh ** -0.5
    for h in range(H):
        sbq_ref[h] = (p_ref[:, sb_q0 + h * sb_dh:sb_q0 + (h + 1) * sb_dh] * sb_scale).astype(bf16)
        mbq_ref[h] = (p_ref[:, mb_q0 + h * mb_dh:mb_q0 + (h + 1) * mb_dh] * mb_scale).astype(bf16)
    sbk_ref[...] = p_ref[:, sb_k0:sb_k0 + sb_dh]
    sbv_ref[...] = p_ref[:, sb_v0:sb_v0 + sb_dh]
    mbk_ref[...] = p_ref[:, mb_k0:mb_k0 + mb_dh]
    mbv_ref[...] = p_ref[:, mb_v0:mb_v0 + mb_dh]
    ckv_ref[...] = _rms(p_ref[:, kva0:kva0 + kvl], kvan_ref[...])
    kpe_ref[...] = rot(p_ref[:, kpe0:kpe0 + rope])
    cq = _rms(p_ref[:, qa0:qa0 + ql], qan_ref[...]).astype(bf16)
    qn = _dot(cq, wqn_ref[...])
    qr = _dot(cq, wqr_ref[...])
    for h in range(H):
        qlat_ref[h] = _dot(qn[:, h * nope:(h + 1) * nope].astype(bf16), wukt_ref[h]).astype(bf16)
        qpe_ref[h] = rot(qr[:, h * rope:(h + 1) * rope]).astype(bf16)


def postproj(p, cos64, sin64, q_a_norm, kv_a_norm, w_qn, w_qr, w_ukt, dims):
    m, npad = p.shape
    H, sb_dh, mb_dh, ql, kvl, rope, nope = (dims[k] for k in ("H", "sb_dh", "mb_dh", "ql", "kvl", "rope", "nope"))
    tm = _pick(m, (256, 128, 64, 32, 16))
    row = lambda w: pl.BlockSpec((tm, w), lambda i: (i, 0))
    full = lambda a: pl.BlockSpec(a.shape, lambda i: (0,) * a.ndim)
    hd = lambda w: pl.BlockSpec((H, tm, w), lambda i: (0, i, 0))
    qan = q_a_norm.reshape(1, ql).astype(f32)
    kvan = kv_a_norm.reshape(1, kvl).astype(f32)
    out_shape = (
        jax.ShapeDtypeStruct((H, m, sb_dh), bf16), jax.ShapeDtypeStruct((H, m, mb_dh), bf16),
        jax.ShapeDtypeStruct((H, m, kvl), bf16), jax.ShapeDtypeStruct((H, m, rope), bf16),
        jax.ShapeDtypeStruct((m, sb_dh), f32), jax.ShapeDtypeStruct((m, sb_dh), f32),
        jax.ShapeDtypeStruct((m, kvl), f32), jax.ShapeDtypeStruct((m, rope), f32),
        jax.ShapeDtypeStruct((m, mb_dh), f32), jax.ShapeDtypeStruct((m, mb_dh), f32),
    )
    out_specs = (hd(sb_dh), hd(mb_dh), hd(kvl), hd(rope), row(sb_dh), row(sb_dh), row(kvl), row(rope), row(mb_dh), row(mb_dh))
    return pl.pallas_call(
        functools.partial(_postproj_kernel, dims=dims),
        grid=(m // tm,),
        in_specs=[row(npad), row(rope), row(rope), full(qan), full(kvan), full(w_qn), full(w_qr), full(w_ukt)],
        out_specs=out_specs,
        out_shape=out_shape,
        compiler_params=_cp("parallel"),
    )(p, cos64, sin64, qan, kvan, w_qn, w_qr, w_ukt)


def _mixnorm_kernel(osb_ref, olat_ref, omb_ref, wuv_ref, g_ref, o_ref, omla_sc, *, H, vdim):
    sbw = osb_ref.shape[1]
    mbw = omb_ref.shape[1]
    mlaw = H * vdim
    for h in range(H):
        omla_sc[:, h * vdim:(h + 1) * vdim] = _dot(olat_ref[h], wuv_ref[h])
    o_ref[:, :sbw] = _rms(osb_ref[...], g_ref[:, :sbw]).astype(bf16)
    o_ref[:, sbw:sbw + mlaw] = _rms(omla_sc[...], g_ref[:, sbw:sbw + mlaw]).astype(bf16)
    o_ref[:, sbw + mlaw:] = _rms(omb_ref[...], g_ref[:, sbw + mlaw:sbw + mlaw + mbw]).astype(bf16)


def mixnorm(o_sb, o_lat, o_mb, w_uv, g_mix):
    m, sbw = o_sb.shape
    H, _, kvl = o_lat.shape
    vdim = w_uv.shape[2]
    mbw = o_mb.shape[1]
    mix = sbw + H * vdim + mbw
    tm = _pick(m, (256, 128, 64, 32, 16))
    return pl.pallas_call(
        functools.partial(_mixnorm_kernel, H=H, vdim=vdim),
        grid=(m // tm,),
        in_specs=[pl.BlockSpec((tm, sbw), lambda i: (i, 0)), pl.BlockSpec((H, tm, kvl), lambda i: (0, i, 0)),
                  pl.BlockSpec((tm, mbw), lambda i: (i, 0)), pl.BlockSpec(w_uv.shape, lambda i: (0, 0, 0)),
                  pl.BlockSpec((1, mix), lambda i: (0, 0))],
        out_specs=pl.BlockSpec((tm, mix), lambda i: (i, 0)),
        out_shape=jax.ShapeDtypeStruct((m, mix), bf16),
        scratch_shapes=[pltpu.VMEM((tm, H * vdim), f32)],
        compiler_params=_cp("parallel"),
    )(o_sb, o_lat, o_mb, w_uv, g_mix.reshape(1, mix).astype(f32))


def _router_kernel(x_ref, g_ref, wr_ref, h_ref, gates_ref, *, n_exp):
    h = _rms(x_ref[...], g_ref[...])
    h_ref[...] = h
    h_hi, h_lo = _split(h)
    w_hi, w_lo = _split(wr_ref[...])
    logits = _dot(h_hi, w_hi) + (_dot(h_hi, w_lo) + _dot(h_lo, w_hi))
    lane = lax.broadcasted_iota(jnp.int32, logits.shape, 1)
    work = jnp.where(lane < n_exp, logits, -jnp.inf)
    vals, idxs = [], []
    for _ in range(TOP_K):
        mx = jnp.max(work, axis=1, keepdims=True)
        ix = jnp.min(jnp.where(work == mx, lane, LANES), axis=1, keepdims=True)
        vals.append(mx)
        idxs.append(ix)
        work = jnp.where(lane == ix, -jnp.inf, work)
    ex = [jnp.exp(v - vals[0]) for v in vals]
    den = ex[0]
    for e_ in ex[1:]:
        den = den + e_
    info = jnp.zeros_like(logits)
    for k_, (e_, ix) in enumerate(zip(ex, idxs)):
        info = jnp.where(lane == k_, ix.astype(f32), info)
        info = jnp.where(lane == TOP_K + k_, e_ / den, info)
    gates_ref[...] = info


def router(x, g, w_router):
    m, d = x.shape
    n_exp = w_router.shape[1]
    wr = jnp.pad(w_router.astype(f32), ((0, 0), (0, LANES - n_exp)))
    tm = _pick(m, (256, 128, 64, 32, 16))
    return pl.pallas_call(
        functools.partial(_router_kernel, n_exp=n_exp),
        grid=(m // tm,),
        in_specs=[pl.BlockSpec((tm, d), lambda i: (i, 0)), pl.BlockSpec((1, d), lambda i: (0, 0)),
                  pl.BlockSpec((d, LANES), lambda i: (0, 0))],
        out_specs=(pl.BlockSpec((tm, d), lambda i: (i, 0)), pl.BlockSpec((tm, LANES), lambda i: (i, 0))),
        out_shape=(jax.ShapeDtypeStruct((m, d), f32), jax.ShapeDtypeStruct((m, LANES), f32)),
        compiler_params=_cp("parallel"),
    )(x, g.reshape(1, d).astype(f32), wr)


MOE_ROW_TILE = 256


def _dispatch_plan(expert_ids, n_exp):
    m, k = expert_ids.shape
    tile = MOE_ROW_TILE
    n_rows = -(-(m * k) // tile) * tile + n_exp * tile
    flat = expert_ids.reshape(m * k)
    onehot = (flat[:, None] == jnp.arange(n_exp, dtype=jnp.int32)[None, :]).astype(jnp.int32)
    rank = jnp.sum((jnp.cumsum(onehot, axis=0) - onehot) * onehot, axis=1)
    padded = -(-jnp.sum(onehot, axis=0) // tile) * tile
    ends = jnp.cumsum(padded)
    pos = (ends - padded)[flat] + rank
    n_tiles = n_rows // tile
    tile_start = jnp.arange(n_tiles, dtype=jnp.int32) * tile
    tile_expert = jnp.minimum(jnp.sum((tile_start[:, None] >= ends[None, :]).astype(jnp.int32), axis=1), n_exp - 1)
    n_used = (ends[-1] // tile).astype(jnp.int32).reshape(1)
    src_tok = jnp.zeros((n_rows,), jnp.int32).at[pos].set(jnp.arange(m * k, dtype=jnp.int32) // k)
    return pos.astype(jnp.int32), src_tok, tile_expert.astype(jnp.int32), n_used, n_rows


def _gather_rows_kernel(src_ref, nused_ref, h_hbm, o_ref, buf, sem, *, tile):
    r = pl.program_id(0)

    @pl.when(r < nused_ref[0])
    def _():
        def start(i, _):
            pltpu.make_async_copy(h_hbm.at[pl.ds(src_ref[r * tile + i], 1), :], buf.at[pl.ds(i, 1), :], sem.at[0]).start()
            return 0
        lax.fori_loop(0, tile, start, 0)

        def wait(i, _):
            pltpu.make_async_copy(h_hbm.at[pl.ds(0, 1), :], buf.at[pl.ds(i, 1), :], sem.at[0]).wait()
            return 0
        lax.fori_loop(0, tile, wait, 0)
        o_ref[...] = buf[...].astype(o_ref.dtype)

    @pl.when(r >= nused_ref[0])
    def _():
        o_ref[...] = jnp.zeros_like(o_ref)


def gather_rows(h, src_tok, n_used, n_rows):
    m, d = h.shape
    tile = MOE_ROW_TILE
    return pl.pallas_call(
        functools.partial(_gather_rows_kernel, tile=tile),
        grid_spec=pltpu.PrefetchScalarGridSpec(
            num_scalar_prefetch=2, grid=(n_rows // tile,),
            in_specs=[pl.BlockSpec(memory_space=pl.ANY)],
            out_specs=pl.BlockSpec((tile, d), lambda r, src, nu: (r, 0)),
            scratch_shapes=[pltpu.VMEM((tile, d), f32), pltpu.SemaphoreType.DMA((1,))]),
        out_shape=jax.ShapeDtypeStruct((n_rows, d), bf16),
        compiler_params=_cp("arbitrary"),
    )(src_tok, n_used, h)


def _new_expert_tile(te_ref, nused_ref):
    r = pl.program_id(1)
    changed = jnp.logical_or(r == 0, te_ref[r] != te_ref[jnp.maximum(r - 1, 0)])
    return jnp.logical_and(changed, r < nused_ref[0])


def _moe_gateup_kernel(te_ref, nused_ref, h_ref, wg_ref, wu_ref, o_ref, wg_sc, wu_sc):
    @pl.when(_new_expert_tile(te_ref, nused_ref))
    def _():
        wg_sc[...] = wg_ref[...].astype(bf16)
        wu_sc[...] = wu_ref[...].astype(bf16)

    @pl.when(pl.program_id(1) < nused_ref[0])
    def _():
        h = h_ref[...]
        o_ref[...] = _swiglu(_dot(h, wg_sc[...]), _dot(h, wu_sc[...])).astype(o_ref.dtype)

    @pl.when(pl.program_id(1) >= nused_ref[0])
    def _():
        o_ref[...] = jnp.zeros_like(o_ref)


def moe_gateup(hs, wg, wu, tile_expert, n_used):
    n_rows, d = hs.shape
    f = wg.shape[2]
    tile = MOE_ROW_TILE
    tn = _pick(f, (512, 256, 128))
    row = lambda j, r, te, nu: (jnp.minimum(r, nu[0] - 1), 0)
    wmap = lambda j, r, te, nu: (te[jnp.minimum(r, nu[0] - 1)], 0, j)
    return pl.pallas_call(
        _moe_gateup_kernel,
        grid_spec=pltpu.PrefetchScalarGridSpec(
            num_scalar_prefetch=2, grid=(f // tn, n_rows // tile),
            in_specs=[pl.BlockSpec((tile, d), row), pl.BlockSpec((None, d, tn), wmap), pl.BlockSpec((None, d, tn), wmap)],
            out_specs=pl.BlockSpec((tile, tn), lambda j, r, te, nu: (r, j)),
            scratch_shapes=[pltpu.VMEM((d, tn), bf16), pltpu.VMEM((d, tn), bf16)]),
        out_shape=jax.ShapeDtypeStruct((n_rows, f), bf16),
        compiler_params=_cp("arbitrary", "arbitrary"),
    )(tile_expert, n_used, hs, wg, wu)


def _moe_down_kernel(te_ref, nused_ref, a_ref, wd_ref, o_ref, wd_sc):
    @pl.when(_new_expert_tile(te_ref, nused_ref))
    def _():
        wd_sc[...] = wd_ref[...].astype(bf16)

    @pl.when(pl.program_id(1) < nused_ref[0])
    def _():
        o_ref[...] = _dot(a_ref[...], wd_sc[...])

    @pl.when(pl.program_id(1) >= nused_ref[0])
    def _():
        o_ref[...] = jnp.zeros_like(o_ref)


def moe_down(a, wd, tile_expert, n_used):
    n_rows, f = a.shape
    d = wd.shape[2]
    tile = MOE_ROW_TILE
    tn = _pick(d, (512, 256, 128))
    return pl.pallas_call(
        _moe_down_kernel,
        grid_spec=pltpu.PrefetchScalarGridSpec(
            num_scalar_prefetch=2, grid=(d // tn, n_rows // tile),
            in_specs=[pl.BlockSpec((tile, f), lambda j, r, te, nu: (jnp.minimum(r, nu[0] - 1), 0)),
                      pl.BlockSpec((None, f, tn), lambda j, r, te, nu: (te[jnp.minimum(r, nu[0] - 1)], 0, j))],
            out_specs=pl.BlockSpec((tile, tn), lambda j, r, te, nu: (r, j)),
            scratch_shapes=[pltpu.VMEM((f, tn), bf16)]),
        out_shape=jax.ShapeDtypeStruct((n_rows, d), f32),
        compiler_params=_cp("arbitrary", "arbitrary"),
    )(tile_expert, n_used, a, wd)


def _moe_combine_kernel(pos_ref, x_ref, w_ref, y_hbm, o_ref, buf, sem, *, tile, k):
    i0 = pl.program_id(0) * tile

    def start(i, _):
        for kk in range(k):
            pltpu.make_async_copy(y_hbm.at[pl.ds(pos_ref[(i0 + i) * k + kk], 1), :], buf.at[kk, pl.ds(i, 1), :],
                                  sem.at[kk]).start()
        return 0
    lax.fori_loop(0, tile, start, 0)

    def wait(i, _):
        for kk in range(k):
            pltpu.make_async_copy(y_hbm.at[pl.ds(0, 1), :], buf.at[kk, pl.ds(i, 1), :], sem.at[kk]).wait()
        return 0
    lax.fori_loop(0, tile, wait, 0)
    out = x_ref[...]
    for kk in range(k):
        out = out + w_ref[:, kk:kk + 1] * buf[kk]
    o_ref[...] = out


def moe_combine(x, y, pos, weights):
    m, d = x.shape
    k = weights.shape[1]
    tile = _pick(m, (128, 64, 32, 16, 8))
    return pl.pallas_call(
        functools.partial(_moe_combine_kernel, tile=tile, k=k),
        grid_spec=pltpu.PrefetchScalarGridSpec(
            num_scalar_prefetch=1, grid=(m // tile,),
            in_specs=[pl.BlockSpec((tile, d), lambda i, p: (i, 0)), pl.BlockSpec((tile, k), lambda i, p: (i, 0)),
                      pl.BlockSpec(memory_space=pl.ANY)],
            out_specs=pl.BlockSpec((tile, d), lambda i, p: (i, 0)),
            scratch_shapes=[pltpu.VMEM((k, tile, d), f32), pltpu.SemaphoreType.DMA((k,))]),
        out_shape=jax.ShapeDtypeStruct((m, d), f32),
        compiler_params=_cp("arbitrary"),
    )(pos, x, weights, y)


def _upper(tk):
    j = lax.broadcasted_iota(jnp.int32, (tk, tk), 0)
    s = lax.broadcasted_iota(jnp.int32, (tk, tk), 1)
    return jnp.where(j > s, 1.0, 0.0).astype(bf16)


def _sb_block(q, k, v, causal, carry, acc):
    rows = q.shape[0]
    tk = k.shape[0]
    z = _dot_t(q, k)
    sp = jnp.maximum(z, 0.0) + jnp.log1p(jnp.exp(-jnp.abs(z)))
    l1 = jnp.where(causal, -sp, 0.0)
    hi, lo = _split(l1)
    cum = _dot(jnp.concatenate([hi, lo], axis=0), _upper(tk))
    later = carry + (cum[:rows] + cum[rows:])
    w = jnp.where(causal, jnp.exp(z - sp + later), 0.0)
    acc = acc + _dot(w.astype(bf16), v)
    carry = carry + jnp.sum(l1, axis=1, keepdims=True)
    return carry, acc


def _sb_past_blocks(q, kt, vt, carry, acc, nblk, tk):
    rows = q.shape[0]
    z = _dot(q, kt)
    zs = jnp.concatenate([z[:, b * tk:(b + 1) * tk] for b in range(nblk)], axis=0)
    sp = jnp.maximum(zs, 0.0) + jnp.log1p(jnp.exp(-jnp.abs(zs)))
    l1 = -sp
    hi, lo = _split(l1)
    cum = _dot(jnp.concatenate([hi, lo], axis=0), _upper(tk))
    cum = cum[:nblk * rows] + cum[nblk * rows:]
    tot = jnp.sum(l1, axis=1, keepdims=True)
    carries = [None] * nblk
    for b in reversed(range(nblk)):
        carries[b] = carry
        carry = carry + tot[b * rows:(b + 1) * rows]
    w = jnp.exp(zs - sp + cum + jnp.concatenate(carries, axis=0))
    wl = jnp.concatenate([w[b * rows:(b + 1) * rows] for b in range(nblk)], axis=1).astype(bf16)
    return carry, acc + _dot_t(wl, vt)


def _sb_prompt_kernel(q_ref, k_ref, v_ref, o_ref, carry_sc, acc_sc, *, H, tq, tk):
    qi = pl.program_id(1)
    st = pl.program_id(2)
    dh = q_ref.shape[2]
    rows = H * tq
    kb = (qi * tq + tq - 1) // tk - st

    @pl.when(st == 0)
    def _():
        carry_sc[...] = jnp.zeros_like(carry_sc)
        acc_sc[...] = jnp.zeros_like(acc_sc)

    @pl.when(kb >= 0)
    def _():
        q = q_ref[...].reshape(rows, dh)
        qpos = qi * tq + lax.broadcasted_iota(jnp.int32, (rows, tk), 0) % tq
        kpos = kb * tk + lax.broadcasted_iota(jnp.int32, (rows, tk), 1)
        carry, acc = _sb_block(q, k_ref[...].astype(bf16), v_ref[...].astype(bf16), kpos < qpos,
                               carry_sc[...], acc_sc[...])
        carry_sc[...] = carry
        acc_sc[...] = acc

    @pl.when(st == pl.num_programs(2) - 1)
    def _():
        for h in range(H):
            o_ref[:, h * dh:(h + 1) * dh] = acc_sc[h * tq:(h + 1) * tq, :]


def sb_prompt(q, k, v, B, T):
    H, _, dh = q.shape
    tq = _pick(T, (128, 64, 32, 16))
    tk = _pick(T, (256, 128, 64, 32, 16))
    nq, nk = T // tq, T // tk

    def kv_map(b, qi, st):
        return (b * nk + jnp.maximum((qi * tq + tq - 1) // tk - st, 0), 0)

    return pl.pallas_call(
        functools.partial(_sb_prompt_kernel, H=H, tq=tq, tk=tk),
        grid=(B, nq, nk),
        in_specs=[pl.BlockSpec((H, tq, dh), lambda b, qi, st: (0, b * nq + qi, 0)),
                  pl.BlockSpec((tk, dh), kv_map), pl.BlockSpec((tk, dh), kv_map)],
        out_specs=pl.BlockSpec((tq, H * dh), lambda b, qi, st: (b * nq + qi, 0)),
        out_shape=jax.ShapeDtypeStruct((B * T, H * dh), f32),
        scratch_shapes=[pltpu.VMEM((H * tq, 1), f32), pltpu.VMEM((H * tq, dh), f32)],
        compiler_params=_cp("parallel", "parallel", "arbitrary"),
    )(q, k, v)


def _page_dst(buf, slot, j, page, transposed):
    if transposed:
        return buf.at[slot, :, pl.ds(j * page, page)]
    return buf.at[slot, pl.ds(j * page, page), :]


def _paged_start(pt_ref, seq, first_page, n_pages, page, layer, pairs, slot):
    def body(j, _):
        pg = pt_ref[seq, first_page + j]
        for cache, buf, sem, tr in pairs:
            pltpu.make_async_copy(cache.at[layer, pg], _page_dst(buf, slot, j, page, tr), sem.at[slot]).start()
        return 0
    lax.fori_loop(0, n_pages, body, 0)


def _paged_wait(n_pages, page, layer, pairs, slot):
    def body(j, _):
        for cache, buf, sem, tr in pairs:
            pltpu.make_async_copy(cache.at[layer, 0], _page_dst(buf, slot, j, page, tr), sem.at[slot]).wait()
        return 0
    lax.fori_loop(0, n_pages, body, 0)


def _sb_sample_kernel(pt_ref, q_ref, kn_ref, vn_ref, kc_hbm, vc_hbm, o_ref, kbuf, vbuf, ksem, vsem,
                      *, layer, n_pages, page, DT, tk):
    b = pl.program_id(0)
    nb = pl.num_programs(0)
    slot = b % 2
    pairs = [(kc_hbm, kbuf, ksem, True), (vc_hbm, vbuf, vsem, True)]

    @pl.when(b == 0)
    def _():
        _paged_start(pt_ref, 0, 0, n_pages, page, layer, pairs, 0)

    @pl.when(b + 1 < nb)
    def _():
        _paged_start(pt_ref, b + 1, 0, n_pages, page, layer, pairs, 1 - slot)

    _paged_wait(n_pages, page, layer, pairs, slot)

    q = q_ref[0]
    rows, dh = q.shape
    t_idx = lax.broadcasted_iota(jnp.int32, (rows, NEW_PAD), 0) % DT
    j_idx = lax.broadcasted_iota(jnp.int32, (rows, NEW_PAD), 1)
    carry, acc = _sb_block(q, kn_ref[0].astype(bf16), vn_ref[0].astype(bf16), j_idx < t_idx,
                           jnp.zeros((rows, 1), f32), jnp.zeros((rows, dh), f32))
    n_grp = (n_pages * page) // tk
    per = _pick(n_grp, (16, 8, 4, 2, 1))
    for sb in reversed(range(n_grp // per)):
        lo, hi = sb * per * tk, (sb + 1) * per * tk
        carry, acc = _sb_past_blocks(q, kbuf[slot, :, lo:hi].astype(bf16), vbuf[slot, :, lo:hi].astype(bf16),
                                     carry, acc, per, tk)
    o_ref[0] = acc


def _sample_call(kernel, pt, blocked, caches, out_shape, out_block, scratch, grid, n_prefetch=1):
    nd = len(grid)
    def imap(shape):
        if nd == 1:
            return lambda b, pt_: (b,) + (0,) * (len(shape) - 1)
        return lambda b, c, pt_: (b,) + (0,) * (len(shape) - 1)
    in_specs = [pl.BlockSpec((1,) + a.shape[1:], imap(a.shape)) for a in blocked]
    in_specs += [pl.BlockSpec(memory_space=pl.ANY) for _ in caches]
    return pl.pallas_call(
        kernel,
        grid_spec=pltpu.PrefetchScalarGridSpec(
            num_scalar_prefetch=n_prefetch, grid=grid, in_specs=in_specs,
            out_specs=pl.BlockSpec(out_block, imap(out_shape.shape)), scratch_shapes=scratch),
        out_shape=out_shape,
        compiler_params=_cp(*(("arbitrary",) * nd)),
    )(pt, *blocked, *caches)


def sb_sample(q, k_new, v_new, cache_k, cache_v, page_table, layer, DT):
    DB, rows, dh = q.shape
    n_pages = page_table.shape[1]
    page = cache_k.shape[3]
    tk = 2 * page
    assert n_pages % 2 == 0
    P = n_pages * page
    scratch = [pltpu.VMEM((2, dh, P), f32), pltpu.VMEM((2, dh, P), f32),
               pltpu.SemaphoreType.DMA((2,)), pltpu.SemaphoreType.DMA((2,))]
    return _sample_call(
        functools.partial(_sb_sample_kernel, layer=layer, n_pages=n_pages, page=page, DT=DT, tk=tk),
        page_table, [q, k_new, v_new], [cache_k, cache_v],
        jax.ShapeDtypeStruct((DB, rows, dh), f32), (1, rows, dh), scratch, (DB,))


def _softmax_step(s, v, m_prev, l_prev, acc_prev, mask=None, v_transposed=False):
    if mask is not None:
        s = jnp.where(mask, s, NEG)
    m_new = jnp.maximum(m_prev, jnp.max(s, axis=1, keepdims=True))
    alpha = jnp.exp(m_prev - m_new)
    p = jnp.exp(s - m_new)
    if mask is not None:
        p = jnp.where(mask, p, 0.0)
    l_new = alpha * l_prev + jnp.sum(p, axis=1, keepdims=True)
    pb = p.astype(bf16)
    acc_new = alpha * acc_prev + (_dot_t(pb, v) if v_transposed else _dot(pb, v))
    return m_new, l_new, acc_new


def _mla_prompt_kernel(ql_ref, qp_ref, ckv_ref, kpe_ref, o_ref, m_sc, l_sc, acc_sc, *, H, tq, tk, scale):
    qi = pl.program_id(1)
    kj = pl.program_id(2)
    rows = H * tq
    last = (qi * tq + tq - 1) // tk

    @pl.when(kj == 0)
    def _():
        m_sc[...] = jnp.full_like(m_sc, NEG)
        l_sc[...] = jnp.zeros_like(l_sc)
        acc_sc[...] = jnp.zeros_like(acc_sc)

    @pl.when(kj <= last)
    def _():
        ck = ckv_ref[...].astype(bf16)
        kp = kpe_ref[...].astype(bf16)
        hg = _pick(H, (HEAD_GROUP, 2, 1))
        gr = hg * tq
        qpos = qi * tq + lax.broadcasted_iota(jnp.int32, (gr, tk), 0) % tq
        kpos = kj * tk + lax.broadcasted_iota(jnp.int32, (gr, tk), 1)
        mask = kpos <= qpos
        for g in range(H // hg):
            rs = slice(g * gr, (g + 1) * gr)
            ql = ql_ref[g * hg:(g + 1) * hg].reshape(gr, ql_ref.shape[2])
            qp = qp_ref[g * hg:(g + 1) * hg].reshape(gr, qp_ref.shape[2])
            s = (_dot_t(ql, ck) + _dot_t(qp, kp)) * scale
            m, l, acc = _softmax_step(s, ck, m_sc[rs], l_sc[rs], acc_sc[rs], mask)
            m_sc[rs] = m
            l_sc[rs] = l
            acc_sc[rs] = acc

    @pl.when(kj == pl.num_programs(2) - 1)
    def _():
        o_ref[...] = (acc_sc[...] / l_sc[...]).reshape(o_ref.shape).astype(o_ref.dtype)


def mla_prompt(q_lat, q_pe, ckv, kpe, B, T, scale):
    H, _, C = q_lat.shape
    R = q_pe.shape[2]
    tq = _pick(T, (128, 64, 32, 16))
    tk = _pick(T, (512, 256, 128, 64, 32, 16))
    nq, nk = T // tq, T // tk

    def kv_map(b, qi, kj):
        return (b * nk + jnp.minimum(kj, (qi * tq + tq - 1) // tk), 0)

    q_map = lambda b, qi, kj: (0, b * nq + qi, 0)
    return pl.pallas_call(
        functools.partial(_mla_prompt_kernel, H=H, tq=tq, tk=tk, scale=scale),
        grid=(B, nq, nk),
        in_specs=[pl.BlockSpec((H, tq, C), q_map), pl.BlockSpec((H, tq, R), q_map),
                  pl.BlockSpec((tk, C), kv_map), pl.BlockSpec((tk, R), kv_map)],
        out_specs=pl.BlockSpec((H, tq, C), q_map),
        out_shape=jax.ShapeDtypeStruct((H, B * T, C), bf16),
        scratch_shapes=[pltpu.VMEM((H * tq, 1), f32), pltpu.VMEM((H * tq, 1), f32), pltpu.VMEM((H * tq, C), f32)],
        compiler_params=_cp("parallel", "parallel", "arbitrary"),
    )(q_lat, q_pe, ckv, kpe)


def _mla_sample_kernel(pt_ref, ql_ref, qp_ref, cn_ref, pn_ref, ckv_hbm, kpe_hbm, o_ref,
                       cbuf, pbuf, csem, psem, m_sc, l_sc, acc_sc, *, layer, chunk, page, DT, scale):
    b = pl.program_id(0)
    c = pl.program_id(1)
    nc = pl.num_programs(1)
    step = b * nc + c
    total = pl.num_programs(0) * nc
    slot = step % 2
    pairs = [(ckv_hbm, cbuf, csem, False), (kpe_hbm, pbuf, psem, True)]

    @pl.when(step == 0)
    def _():
        _paged_start(pt_ref, 0, 0, chunk, page, layer, pairs, 0)

    @pl.when(step + 1 < total)
    def _():
        nxt = step + 1
        _paged_start(pt_ref, nxt // nc, (nxt % nc) * chunk, chunk, page, layer, pairs, 1 - slot)

    _paged_wait(chunk, page, layer, pairs, slot)

    ql = ql_ref[0]
    qp = qp_ref[0]
    rows = ql.shape[0]

    @pl.when(c == 0)
    def _():
        cn = cn_ref[0].astype(bf16)
        pn = pn_ref[0].astype(bf16)
        s = (_dot_t(ql, cn) + _dot_t(qp, pn)) * scale
        t_idx = lax.broadcasted_iota(jnp.int32, (rows, NEW_PAD), 0) % DT
        j_idx = lax.broadcasted_iota(jnp.int32, (rows, NEW_PAD), 1)
        m, l, acc = _softmax_step(s, cn, jnp.full((rows, 1), NEG, f32), jnp.zeros((rows, 1), f32),
                                  jnp.zeros(acc_sc.shape, f32), j_idx <= t_idx)
        m_sc[...] = m
        l_sc[...] = l
        acc_sc[...] = acc

    width = (chunk * page) // MLA_SAMPLE_PARTS
    states = [(m_sc[...], l_sc[...], acc_sc[...])]
    for i in range(MLA_SAMPLE_PARTS):
        ck = cbuf[slot, i * width:(i + 1) * width, :].astype(bf16)
        kpt = pbuf[slot, :, i * width:(i + 1) * width].astype(bf16)
        s = (_dot_t(ql, ck) + _dot(qp, kpt)) * scale
        m_i = jnp.max(s, axis=1, keepdims=True)
        p = jnp.exp(s - m_i)
        states.append((m_i, jnp.sum(p, axis=1, keepdims=True), _dot(p.astype(bf16), ck)))
    m = states[0][0]
    for st in states[1:]:
        m = jnp.maximum(m, st[0])
    l = jnp.zeros_like(m)
    acc = jnp.zeros(acc_sc.shape, f32)
    for m_i, l_i, acc_i in states:
        w_i = jnp.exp(m_i - m)
        l = l + w_i * l_i
        acc = acc + w_i * acc_i
    m_sc[...] = m
    l_sc[...] = l
    acc_sc[...] = acc

    @pl.when(c == nc - 1)
    def _():
        o_ref[0] = (acc_sc[...] / l_sc[...]).astype(o_ref.dtype)


def mla_sample(q_lat, q_pe, ckv_new, kpe_new, cache_ckv, cache_kpe, page_table, layer, DT, scale):
    DB, rows, C = q_lat.shape
    R = q_pe.shape[2]
    n_pages = page_table.shape[1]
    page = cache_ckv.shape[2]
    chunk = _pick(n_pages, (32, 16, 8, 4, 2, 1))
    scratch = [pltpu.VMEM((2, chunk * page, C), f32), pltpu.VMEM((2, R, chunk * page), f32),
               pltpu.SemaphoreType.DMA((2,)), pltpu.SemaphoreType.DMA((2,)),
               pltpu.VMEM((rows, 1), f32), pltpu.VMEM((rows, 1), f32), pltpu.VMEM((rows, C), f32)]
    return _sample_call(
        functools.partial(_mla_sample_kernel, layer=layer, chunk=chunk, page=page, DT=DT, scale=scale),
        page_table, [q_lat, q_pe, ckv_new, kpe_new], [cache_ckv, cache_kpe],
        jax.ShapeDtypeStruct((DB, rows, C), bf16), (1, rows, C), scratch, (DB, n_pages // chunk))


def _moba_select(q, kmean, n_valid):
    nb, dh = kmean.shape
    km = jnp.concatenate([kmean, jnp.zeros((LANES - nb, dh), f32)], axis=0) if nb < LANES else kmean
    k_hi, k_lo = _split(km)
    return _moba_topk(_dot_t(q, k_hi) + _dot_t(q, k_lo), n_valid)


def _moba_topk(gate, n_valid):
    blk = lax.broadcasted_iota(jnp.int32, gate.shape, 1)
    valid = blk < n_valid
    work = jnp.where(valid, gate, -jnp.inf)
    sel = jnp.zeros(gate.shape, jnp.bool_)
    for _ in range(MB_TOPK):
        mx = jnp.max(work, axis=1, keepdims=True)
        ix = jnp.min(jnp.where(work == mx, blk, LANES), axis=1, keepdims=True)
        pick = blk == ix
        sel = jnp.logical_or(sel, pick)
        work = jnp.where(pick, -jnp.inf, work)
    return jnp.logical_and(sel, valid)


def _block_means(k, nb):
    return jnp.mean(k.reshape(nb, MB_BLOCK, k.shape[1]), axis=1)


def _moba_prompt_kernel(q_ref, k_ref, v_ref, slope_ref, o_ref, m_sc, l_sc, acc_sc, *, H, tq, nb):
    qi = pl.program_id(1)
    dh = q_ref.shape[2]
    rows = H * tq
    q = q_ref[...].reshape(rows, dh)
    q_blk = (qi * tq) // MB_BLOCK
    sel = _moba_select(q, _block_means(k_ref[...], nb), q_blk)
    sel_f = jnp.where(sel, 1.0, 0.0)
    blk = lax.broadcasted_iota(jnp.int32, sel.shape, 1)
    slope = slope_ref[...]
    qpos = qi * tq + lax.broadcasted_iota(jnp.int32, (rows, MB_BLOCK), 0) % tq
    m_sc[...] = jnp.full_like(m_sc, NEG)
    l_sc[...] = jnp.zeros_like(l_sc)
    acc_sc[...] = jnp.zeros_like(acc_sc)
    for kb in range(nb):
        @pl.when(kb <= q_blk)
        def _(kb=kb):
            kk = k_ref[kb * MB_BLOCK:(kb + 1) * MB_BLOCK, :].astype(bf16)
            vv = v_ref[kb * MB_BLOCK:(kb + 1) * MB_BLOCK, :].astype(bf16)
            kpos = kb * MB_BLOCK + lax.broadcasted_iota(jnp.int32, (rows, MB_BLOCK), 1)
            s = _dot_t(q, kk) - slope * (qpos - kpos).astype(f32)
            chosen = jnp.sum(jnp.where(blk == kb, sel_f, 0.0), axis=1, keepdims=True)
            causal = jnp.where(kpos <= qpos, 1.0, 0.0)
            mask = jnp.where(kb == q_blk, causal, chosen) > 0.5
            m, l, acc = _softmax_step(s, vv, m_sc[...], l_sc[...], acc_sc[...], mask)
            m_sc[...] = m
            l_sc[...] = l
            acc_sc[...] = acc
    out = acc_sc[...] / l_sc[...]
    for h in range(H):
        o_ref[:, h * dh:(h + 1) * dh] = out[h * tq:(h + 1) * tq, :]


def moba_prompt(q, k, v, slopes, B, T):
    H, _, dh = q.shape
    tq = _pick(T, (128, 64, 32, 16))
    assert MB_BLOCK % tq == 0 and T % MB_BLOCK == 0
    nq, nb = T // tq, T // MB_BLOCK
    slope_rows = jnp.repeat(slopes.astype(f32), tq).reshape(H * tq, 1)
    return pl.pallas_call(
        functools.partial(_moba_prompt_kernel, H=H, tq=tq, nb=nb),
        grid=(B, nq),
        in_specs=[pl.BlockSpec((H, tq, dh), lambda b, qi: (0, b * nq + qi, 0)),
                  pl.BlockSpec((T, dh), lambda b, qi: (b, 0)), pl.BlockSpec((T, dh), lambda b, qi: (b, 0)),
                  pl.BlockSpec((H * tq, 1), lambda b, qi: (0, 0))],
        out_specs=pl.BlockSpec((tq, H * dh), lambda b, qi: (b * nq + qi, 0)),
        out_shape=jax.ShapeDtypeStruct((B * T, H * dh), f32),
        scratch_shapes=[pltpu.VMEM((H * tq, 1), f32), pltpu.VMEM((H * tq, 1), f32), pltpu.VMEM((H * tq, dh), f32)],
        compiler_params=_cp("parallel", "arbitrary"),
    )(q, k, v, slope_rows)


def _moba_sample_kernel(pt_ref, q_ref, kn_ref, vn_ref, slope_ref, kc_hbm, vc_hbm, o_ref, kbuf, vbuf, ksem, vsem,
                        *, layer, n_pages, page, DT):
    b = pl.program_id(0)
    nbatch = pl.num_programs(0)
    slot = b % 2
    pairs = [(kc_hbm, kbuf, ksem, True), (vc_hbm, vbuf, vsem, True)]

    @pl.when(b == 0)
    def _():
        _paged_start(pt_ref, 0, 0, n_pages, page, layer, pairs, 0)

    @pl.when(b + 1 < nbatch)
    def _():
        _paged_start(pt_ref, b + 1, 0, n_pages, page, layer, pairs, 1 - slot)

    _paged_wait(n_pages, page, layer, pairs, slot)

    q = q_ref[0]
    rows, dh = q.shape
    P = n_pages * page
    nb = P // MB_BLOCK
    col_id = lax.broadcasted_iota(jnp.int32, (dh, LANES), 1)
    kmt = jnp.zeros((dh, LANES), f32)
    for kb in range(nb):
        col = jnp.sum(kbuf[slot, :, kb * MB_BLOCK:(kb + 1) * MB_BLOCK], axis=1, keepdims=True) * (1.0 / MB_BLOCK)
        kmt = jnp.where(col_id == kb, col, kmt)
    km_hi, km_lo = _split(kmt)
    sel = _moba_topk(_dot(q, km_hi) + _dot(q, km_lo), nb)
    sel_f = jnp.where(sel, 1.0, 0.0)
    blk = lax.broadcasted_iota(jnp.int32, sel.shape, 1)
    slope = slope_ref[...]
    per = _pick(nb, (16, 8, 4, 2, 1))
    width = per * MB_BLOCK
    t_idx = lax.broadcasted_iota(jnp.int32, (rows, width), 0) % DT
    lane = lax.broadcasted_iota(jnp.int32, (rows, width), 1)
    rel = (t_idx - lane).astype(f32)
    m = jnp.full((rows, 1), NEG, f32)
    l = jnp.zeros((rows, 1), f32)
    acc = jnp.zeros((rows, dh), f32)
    for sb in range(nb // per):
        lo = sb * width
        kt = kbuf[slot, :, lo:lo + width].astype(bf16)
        vt = vbuf[slot, :, lo:lo + width].astype(bf16)
        s = _dot(q, kt) - slope * (rel + float(P - lo))
        chosen = [jnp.sum(jnp.where(blk == sb * per + b, sel_f, 0.0), axis=1, keepdims=True) for b in range(per)]
        mask = jnp.concatenate([jnp.broadcast_to(c, (rows, MB_BLOCK)) for c in chosen], axis=1) > 0.5
        m, l, acc = _softmax_step(s, vt, m, l, acc, mask, v_transposed=True)
    t_new = lax.broadcasted_iota(jnp.int32, (rows, NEW_PAD), 0) % DT
    j_new = lax.broadcasted_iota(jnp.int32, (rows, NEW_PAD), 1)
    s = _dot_t(q, kn_ref[0].astype(bf16)) - slope * (t_new - j_new).astype(f32)
    m, l, acc = _softmax_step(s, vn_ref[0].astype(bf16), m, l, acc, j_new <= t_new)
    o_ref[0] = acc / l


def moba_sample(q, k_new, v_new, slopes, cache_k, cache_v, page_table, layer, DT):
    DB, rows, dh = q.shape
    H = rows // DT
    n_pages = page_table.shape[1]
    page = cache_k.shape[3]
    P = n_pages * page
    assert P % MB_BLOCK == 0 and DT < MB_BLOCK and P // MB_BLOCK <= LANES
    slope_rows = jnp.repeat(slopes.astype(f32), DT).reshape(H * DT, 1)
    scratch = [pltpu.VMEM((2, dh, P), f32), pltpu.VMEM((2, dh, P), f32),
               pltpu.SemaphoreType.DMA((2,)), pltpu.SemaphoreType.DMA((2,))]
    nd_map = lambda b, pt_: (0, 0)
    in_specs = [pl.BlockSpec((1, rows, dh), lambda b, pt_: (b, 0, 0)),
                pl.BlockSpec((1, NEW_PAD, dh), lambda b, pt_: (b, 0, 0)),
                pl.BlockSpec((1, NEW_PAD, dh), lambda b, pt_: (b, 0, 0)),
                pl.BlockSpec((rows, 1), nd_map),
                pl.BlockSpec(memory_space=pl.ANY), pl.BlockSpec(memory_space=pl.ANY)]
    return pl.pallas_call(
        functools.partial(_moba_sample_kernel, layer=layer, n_pages=n_pages, page=page, DT=DT),
        grid_spec=pltpu.PrefetchScalarGridSpec(
            num_scalar_prefetch=1, grid=(DB,), in_specs=in_specs,
            out_specs=pl.BlockSpec((1, rows, dh), lambda b, pt_: (b, 0, 0)), scratch_shapes=scratch),
        out_shape=jax.ShapeDtypeStruct((DB, rows, dh), f32),
        compiler_params=_cp("arbitrary"),
    )(page_table, q, k_new, v_new, slope_rows, cache_k, cache_v)


def _rope_tables(pos, rope):
    inv = ROPE_THETA ** (-jnp.arange(0, rope, 2, dtype=f32) / rope)
    ang = pos.astype(f32)[:, None] * inv[None, :]
    cos, sin = jnp.cos(ang), jnp.sin(ang)
    return jnp.concatenate([cos, cos], axis=-1), jnp.concatenate([-sin, sin], axis=-1)


def kernel(x_prompt, x_sample, cache_sb_k, cache_sb_v, cache_mla_ckv, cache_mla_kpe, cache_moba_k, cache_moba_v,
           page_table, ln_mix, w_in, q_a_norm, w_q_b, kv_a_norm, w_kv_b, g_mix, w_out, ln_ff,
           w_ff_gate, w_ff_up, w_ff_down, w_router, w_ex_gate, w_ex_up, w_ex_down, ln_final):
    B, T, D = x_prompt.shape
    DB, DT, _ = x_sample.shape
    depth = w_in.shape[0]
    in_width = w_in.shape[2]
    H = w_q_b.shape[2]
    sb_dh = cache_sb_k.shape[3]
    mb_dh = cache_moba_k.shape[3]
    kvl = cache_mla_ckv.shape[3]
    rope = cache_mla_kpe.shape[3]
    ql = q_a_norm.shape[1]
    nope = w_q_b.shape[3] - rope
    vdim = w_kv_b.shape[3] - nope
    dims = dict(H=H, sb_dh=sb_dh, mb_dh=mb_dh, ql=ql, kvl=kvl, rope=rope, nope=nope)
    assert in_width == H * sb_dh + 2 * sb_dh + ql + kvl + rope + H * mb_dh + 2 * mb_dh
    n_pages = page_table.shape[1]
    page = cache_sb_k.shape[2]
    P = n_pages * page
    Mp, Ms = B * T, DB * DT
    mla_scale = (nope + rope) ** -0.5
    slopes = 2.0 ** (-8.0 * jnp.arange(1, H + 1, dtype=f32) / H)

    pos = jnp.concatenate([jnp.tile(jnp.arange(T, dtype=jnp.int32), B),
                           jnp.tile(P + jnp.arange(DT, dtype=jnp.int32), DB)])
    cos64, sin64 = _rope_tables(pos, rope)
    in_pad = (-in_width) % 512

    def to_seq(a):
        w = a.shape[-1]
        return a.reshape(H, DB, DT, w).transpose(1, 0, 2, 3).reshape(DB, H * DT, w)

    def from_seq_heads(a):
        w = a.shape[-1]
        return a.reshape(DB, H, DT, w).transpose(1, 0, 2, 3).reshape(H, Ms, w)

    def from_seq_rows(a):
        w = a.shape[-1]
        return a.reshape(DB, H, DT, w).transpose(0, 2, 1, 3).reshape(Ms, H * w)

    def new_pad(a):
        return jnp.pad(a.reshape(DB, DT, a.shape[-1]), ((0, 0), (0, NEW_PAD - DT), (0, 0)))

    sbk_t, sbv_t, kpe_t, mbk_t, mbv_t = (jnp.swapaxes(c, 2, 3) for c in
                                         (cache_sb_k, cache_sb_v, cache_mla_kpe, cache_moba_k, cache_moba_v))

    x = jnp.concatenate([x_prompt.reshape(Mp, D), x_sample.reshape(Ms, D)], axis=0)
    rows_p = [[] for _ in range(6)]
    rows_s = [[] for _ in range(6)]
    for l in range(depth):
        w_in_l = jnp.pad(w_in[l], ((0, 0), (0, in_pad))).astype(bf16)
        w_qn = w_q_b[l][:, :, :nope].reshape(ql, H * nope).astype(bf16)
        w_qr = w_q_b[l][:, :, nope:].reshape(ql, H * rope).astype(bf16)
        w_ukt = jnp.transpose(w_kv_b[l][:, :, :nope], (1, 2, 0)).astype(bf16)
        w_uv = jnp.transpose(w_kv_b[l][:, :, nope:], (1, 0, 2)).astype(bf16)

        h = rmsnorm(x, ln_mix[l], bf16)
        p = matmul(h, w_in_l)
        (sbq, mbq, qlat, qpe, sb_k, sb_v, ckv, kpe, mb_k, mb_v) = postproj(
            p, cos64, sin64, q_a_norm[l], kv_a_norm[l], w_qn, w_qr, w_ukt, dims)

        o_sb_p = sb_prompt(sbq, sb_k, sb_v, B, T)
        o_lat_p = mla_prompt(qlat, qpe, ckv, kpe, B, T, mla_scale)
        o_mb_p = moba_prompt(mbq, mb_k, mb_v, slopes, B, T)

        o_sb_s = sb_sample(to_seq(sbq[:, Mp:]), new_pad(sb_k[Mp:]), new_pad(sb_v[Mp:]),
                           sbk_t, sbv_t, page_table, l, DT)
        o_lat_s = mla_sample(to_seq(qlat[:, Mp:]), to_seq(qpe[:, Mp:]), new_pad(ckv[Mp:]), new_pad(kpe[Mp:]),
                             cache_mla_ckv, kpe_t, page_table, l, DT, mla_scale)
        o_mb_s = moba_sample(to_seq(mbq[:, Mp:]), new_pad(mb_k[Mp:]), new_pad(mb_v[Mp:]), slopes,
                             mbk_t, mbv_t, page_table, l, DT)

        o_sb = jnp.concatenate([o_sb_p, from_seq_rows(o_sb_s)], axis=0)
        o_lat = jnp.concatenate([o_lat_p, from_seq_heads(o_lat_s)], axis=1)
        o_mb = jnp.concatenate([o_mb_p, from_seq_rows(o_mb_s)], axis=0)
        cat = mixnorm(o_sb, o_lat, o_mb, w_uv, g_mix[l])
        x = matmul(cat, w_out[l].astype(bf16), residual=x)

        i = l // 2
        if l % 2 == 0:
            h2 = rmsnorm(x, ln_ff[l], bf16)
            a = gateup(h2, w_ff_gate[i], w_ff_up[i])
            x = matmul(a, w_ff_down[i].astype(bf16), residual=x)
        else:
            h2, info = router(x, ln_ff[l], w_router[i])
            n_exp = w_router.shape[2]
            pos, src_tok, tile_expert, n_used, n_rows = _dispatch_plan(info[:, :TOP_K].astype(jnp.int32), n_exp)
            hs = gather_rows(h2, src_tok, n_used, n_rows)
            a = moe_gateup(hs, w_ex_gate[i], w_ex_up[i], tile_expert, n_used)
            y = moe_down(a, w_ex_down[i], tile_expert, n_used)
            x = moe_combine(x, y, pos, info[:, TOP_K:2 * TOP_K])

        for idx, arr in enumerate((sb_k, sb_v, ckv, kpe, mb_k, mb_v)):
            rows_p[idx].append(arr[:Mp].reshape(B, T, arr.shape[-1]))
            rows_s[idx].append(arr[Mp:].reshape(DB, DT, arr.shape[-1]))

    y = rmsnorm(x, ln_final, f32)
    y_prompt = y[:Mp].reshape(B, T, D)
    y_sample = y[Mp:].reshape(DB, DT, D)
    return (y_prompt, y_sample, *[jnp.stack(r, 0) for r in rows_p], *[jnp.stack(r, 0) for r in rows_s])
```

```python
import functools
import math

import jax
import jax.numpy as jnp
from jax import lax
from jax.experimental import pallas as pl
from jax.experimental.pallas import tpu as pltpu

MB_BLOCK = 256
MB_TOPK = 3
TOP_K = 2
ROPE_THETA = 10000.0
NORM_EPS = 1e-6

V7X_VMEM_LIMIT_BYTES = 56 * 1024 * 1024
LANES = 128
NEG = -1e30
NEW_PAD = 128
HEAD_GROUP = 4
MLA_SAMPLE_PARTS = 2

bf16 = jnp.bfloat16
f32 = jnp.float32


def _cp(*sem):
    return pltpu.CompilerParams(dimension_semantics=sem, vmem_limit_bytes=V7X_VMEM_LIMIT_BYTES)


def _pick(n, prefs):
    for p in prefs:
        if n % p == 0:
            return p
    return n


def _row_tile(m, cap):
    best = 0
    for t in range(16, min(m, cap) + 1, 16):
        if m % t == 0:
            best = t
    return best or m


def _pick_k(k):
    if k <= 4096:
        return k
    for d in range(2, 65):
        if k % d == 0 and (k // d) % LANES == 0 and k // d <= 6144:
            return k // d
    return k


def _dot(a, b):
    return jnp.dot(a, b, preferred_element_type=f32)


def _dot_t(a, b):
    return lax.dot_general(a, b, (((1,), (1,)), ((), ())), preferred_element_type=f32)


def _split(x):
    hi = x.astype(bf16)
    lo = (x - hi.astype(f32)).astype(bf16)
    return hi, lo


def _rms(x, g):
    return x * lax.rsqrt(jnp.mean(x * x, axis=-1, keepdims=True) + NORM_EPS) * g


def _rmsnorm_kernel(x_ref, g_ref, o_ref):
    o_ref[...] = _rms(x_ref[...].astype(f32), g_ref[...]).astype(o_ref.dtype)


def rmsnorm(x, g, out_dtype):
    m, d = x.shape
    tm = _pick(m, (512, 256, 128, 64, 32, 16))
    return pl.pallas_call(
        _rmsnorm_kernel,
        grid=(m // tm,),
        in_specs=[pl.BlockSpec((tm, d), lambda i: (i, 0)), pl.BlockSpec((1, d), lambda i: (0, 0))],
        out_specs=pl.BlockSpec((tm, d), lambda i: (i, 0)),
        out_shape=jax.ShapeDtypeStruct((m, d), out_dtype),
        compiler_params=_cp("parallel"),
    )(x, g.reshape(1, d).astype(f32))


def _mm_kernel(*refs, nk, has_res):
    if has_res:
        a_ref, w_ref, r_ref, o_ref = refs[:4]
    else:
        a_ref, w_ref, o_ref = refs[:3]
        r_ref = None
    part = _dot(a_ref[...], w_ref[...])

    def finish(v):
        if r_ref is not None:
            v = v + r_ref[...]
        o_ref[...] = v.astype(o_ref.dtype)

    if nk == 1:
        finish(part)
        return
    acc = refs[-1]
    k = pl.program_id(2)

    @pl.when(k == 0)
    def _():
        acc[...] = part

    @pl.when(k > 0)
    def _():
        acc[...] += part

    @pl.when(k == nk - 1)
    def _():
        finish(acc[...])


def matmul(a, w, residual=None, out_dtype=f32):
    m, kdim = a.shape
    n = w.shape[1]
    tm = _pick(m, (512, 256, 128, 64, 32, 16))
    tn = _pick(n, (512, 256, 128))
    tk = _pick_k(kdim)
    nk = kdim // tk
    in_specs = [pl.BlockSpec((tm, tk), lambda i, j, k: (i, k)), pl.BlockSpec((tk, tn), lambda i, j, k: (k, j))]
    args = [a, w]
    if residual is not None:
        in_specs.append(pl.BlockSpec((tm, tn), lambda i, j, k: (i, j)))
        args.append(residual)
    scratch = [pltpu.VMEM((tm, tn), f32)] if nk > 1 else []
    return pl.pallas_call(
        functools.partial(_mm_kernel, nk=nk, has_res=residual is not None),
        grid=(m // tm, n // tn, nk),
        in_specs=in_specs,
        out_specs=pl.BlockSpec((tm, tn), lambda i, j, k: (i, j)),
        out_shape=jax.ShapeDtypeStruct((m, n), out_dtype),
        scratch_shapes=scratch,
        compiler_params=_cp("parallel", "parallel", "arbitrary"),
    )(*args)


def _swiglu(g, u):
    return g / (1.0 + jnp.exp(-g)) * u


def _gateup_kernel(h_ref, wg_ref, wu_ref, o_ref, wg_sc, wu_sc):
    @pl.when(pl.program_id(1) == 0)
    def _():
        wg_sc[...] = wg_ref[...].astype(bf16)
        wu_sc[...] = wu_ref[...].astype(bf16)

    h = h_ref[...]
    o_ref[...] = _swiglu(_dot(h, wg_sc[...]), _dot(h, wu_sc[...])).astype(o_ref.dtype)


def gateup(h, wg, wu):
    m, d = h.shape
    f = wg.shape[1]
    tm = _row_tile(m, 1280)
    tn = _pick(f, (512, 256, 128))
    w_spec = pl.BlockSpec((d, tn), lambda j, i: (0, j))
    return pl.pallas_call(
        _gateup_kernel,
        grid=(f // tn, m // tm),
        in_specs=[pl.BlockSpec((tm, d), lambda j, i: (i, 0)), w_spec, w_spec],
        out_specs=pl.BlockSpec((tm, tn), lambda j, i: (i, j)),
        out_shape=jax.ShapeDtypeStruct((m, f), bf16),
        scratch_shapes=[pltpu.VMEM((d, tn), bf16), pltpu.VMEM((d, tn), bf16)],
        compiler_params=_cp("arbitrary", "arbitrary"),
    )(h, wg, wu)


def _postproj_kernel(p_ref, cos_ref, sin_ref, qan_ref, kvan_ref, wqn_ref, wqr_ref, wukt_ref,
                     sbq_ref, mbq_ref, qlat_ref, qpe_ref, sbk_ref, sbv_ref, ckv_ref, kpe_ref, mbk_ref, mbv_ref,
                     *, dims):
    H, sb_dh, mb_dh, ql, kvl, rope, nope = (dims[k] for k in ("H", "sb_dh", "mb_dh", "ql", "kvl", "rope", "nope"))
    sb_q0 = 0
    sb_k0 = sb_q0 + H * sb_dh
    sb_v0 = sb_k0 + sb_dh
    qa0 = sb_v0 + sb_dh
    kva0 = qa0 + ql
    kpe0 = kva0 + kvl
    mb_q0 = kpe0 + rope
    mb_k0 = mb_q0 + H * mb_dh
    mb_v0 = mb_k0 + mb_dh
    half = rope // 2
    cos = cos_ref[...]
    sin = sin_ref[...]

    def rot(x):
        return x * cos + jnp.concatenate([x[:, half:], x[:, :half]], axis=-1) * sin

    sb_scale = sb_dh ** -0.5
    mb_scale = mb_dh ** -0.5
    for h in range(H):
        sbq_ref[h] = (p_ref[:, sb_q0 + h * sb_dh:sb_q0 + (h + 1) * sb_dh] * sb_scale).astype(bf16)
        mbq_ref[h] = (p_ref[:, mb_q0 + h * mb_dh:mb_q0 + (h + 1) * mb_dh] * mb_scale).astype(bf16)
    sbk_ref[...] = p_ref[:, sb_k0:sb_k0 + sb_dh]
    sbv_ref[...] = p_ref[:, sb_v0:sb_v0 + sb_dh]
    mbk_ref[...] = p_ref[:, mb_k0:mb_k0 + mb_dh]
    mbv_ref[...] = p_ref[:, mb_v0:mb_v0 + mb_dh]
    ckv_ref[...] = _rms(p_ref[:, kva0:kva0 + kvl], kvan_ref[...])
    kpe_ref[...] = rot(p_ref[:, kpe0:kpe0 + rope])
    cq = _rms(p_ref[:, qa0:qa0 + ql], qan_ref[...]).astype(bf16)
    qn = _dot(cq, wqn_ref[...])
    qr = _dot(cq, wqr_ref[...])
    for h in range(H):
        qlat_ref[h] = _dot(qn[:, h * nope:(h + 1) * nope].astype(bf16), wukt_ref[h]).astype(bf16)
        qpe_ref[h] = rot(qr[:, h * rope:(h + 1) * rope]).astype(bf16)


def postproj(p, cos64, sin64, q_a_norm, kv_a_norm, w_qn, w_qr, w_ukt, dims):
    m, npad = p.shape
    H, sb_dh, mb_dh, ql, kvl, rope, nope = (dims[k] for k in ("H", "sb_dh", "mb_dh", "ql", "kvl", "rope", "nope"))
    tm = _pick(m, (256, 128, 64, 32, 16))
    row = lambda w: pl.BlockSpec((tm, w), lambda i: (i, 0))
    full = lambda a: pl.BlockSpec(a.shape, lambda i: (0,) * a.ndim)
    hd = lambda w: pl.BlockSpec((H, tm, w), lambda i: (0, i, 0))
    qan = q_a_norm.reshape(1, ql).astype(f32)
    kvan = kv_a_norm.reshape(1, kvl).astype(f32)
    out_shape = (
        jax.ShapeDtypeStruct((H, m, sb_dh), bf16), jax.ShapeDtypeStruct((H, m, mb_dh), bf16),
        jax.ShapeDtypeStruct((H, m, kvl), bf16), jax.ShapeDtypeStruct((H, m, rope), bf16),
        jax.ShapeDtypeStruct((m, sb_dh), f32), jax.ShapeDtypeStruct((m, sb_dh), f32),
        jax.ShapeDtypeStruct((m, kvl), f32), jax.ShapeDtypeStruct((m, rope), f32),
        jax.ShapeDtypeStruct((m, mb_dh), f32), jax.ShapeDtypeStruct((m, mb_dh), f32),
    )
    out_specs = (hd(sb_dh), hd(mb_dh), hd(kvl), hd(rope), row(sb_dh), row(sb_dh), row(kvl), row(rope), row(mb_dh), row(mb_dh))
    return pl.pallas_call(
        functools.partial(_postproj_kernel, dims=dims),
        grid=(m // tm,),
        in_specs=[row(npad), row(rope), row(rope), full(qan), full(kvan), full(w_qn), full(w_qr), full(w_ukt)],
        out_specs=out_specs,
        out_shape=out_shape,
        compiler_params=_cp("parallel"),
    )(p, cos64, sin64, qan, kvan, w_qn, w_qr, w_ukt)


def _mixnorm_kernel(osb_ref, olat_ref, omb_ref, wuv_ref, g_ref, o_ref, omla_sc, *, H, vdim):
    sbw = osb_ref.shape[1]
    mbw = omb_ref.shape[1]
    mlaw = H * vdim
    for h in range(H):
        omla_sc[:, h * vdim:(h + 1) * vdim] = _dot(olat_ref[h], wuv_ref[h])
    o_ref[:, :sbw] = _rms(osb_ref[...], g_ref[:, :sbw]).astype(bf16)
    o_ref[:, sbw:sbw + mlaw] = _rms(omla_sc[...], g_ref[:, sbw:sbw + mlaw]).astype(bf16)
    o_ref[:, sbw + mlaw:] = _rms(omb_ref[...], g_ref[:, sbw + mlaw:sbw + mlaw + mbw]).astype(bf16)


def mixnorm(o_sb, o_lat, o_mb, w_uv, g_mix):
    m, sbw = o_sb.shape
    H, _, kvl = o_lat.shape
    vdim = w_uv.shape[2]
    mbw = o_mb.shape[1]
    mix = sbw + H * vdim + mbw
    tm = _pick(m, (256, 128, 64, 32, 16))
    return pl.pallas_call(
        functools.partial(_mixnorm_kernel, H=H, vdim=vdim),
        grid=(m // tm,),
        in_specs=[pl.BlockSpec((tm, sbw), lambda i: (i, 0)), pl.BlockSpec((H, tm, kvl), lambda i: (0, i, 0)),
                  pl.BlockSpec((tm, mbw), lambda i: (i, 0)), pl.BlockSpec(w_uv.shape, lambda i: (0, 0, 0)),
                  pl.BlockSpec((1, mix), lambda i: (0, 0))],
        out_specs=pl.BlockSpec((tm, mix), lambda i: (i, 0)),
        out_shape=jax.ShapeDtypeStruct((m, mix), bf16),
        scratch_shapes=[pltpu.VMEM((tm, H * vdim), f32)],
        compiler_params=_cp("parallel"),
    )(o_sb, o_lat, o_mb, w_uv, g_mix.reshape(1, mix).astype(f32))


def _router_kernel(x_ref, g_ref, wr_ref, h_ref, gates_ref, *, n_exp):
    h = _rms(x_ref[...], g_ref[...])
    h_ref[...] = h
    h_hi, h_lo = _split(h)
    w_hi, w_lo = _split(wr_ref[...])
    logits = _dot(h_hi, w_hi) + (_dot(h_hi, w_lo) + _dot(h_lo, w_hi))
    lane = lax.broadcasted_iota(jnp.int32, logits.shape, 1)
    work = jnp.where(lane < n_exp, logits, -jnp.inf)
    vals, idxs = [], []
    for _ in range(TOP_K):
        mx = jnp.max(work, axis=1, keepdims=True)
        ix = jnp.min(jnp.where(work == mx, lane, LANES), axis=1, keepdims=True)
        vals.append(mx)
        idxs.append(ix)
        work = jnp.where(lane == ix, -jnp.inf, work)
    ex = [jnp.exp(v - vals[0]) for v in vals]
    den = ex[0]
    for e_ in ex[1:]:
        den = den + e_
    info = jnp.zeros_like(logits)
    for k_, (e_, ix) in enumerate(zip(ex, idxs)):
        info = jnp.where(lane == k_, ix.astype(f32), info)
        info = jnp.where(lane == TOP_K + k_, e_ / den, info)
    gates_ref[...] = info


def router(x, g, w_router):
    m, d = x.shape
    n_exp = w_router.shape[1]
    wr = jnp.pad(w_router.astype(f32), ((0, 0), (0, LANES - n_exp)))
    tm = _pick(m, (256, 128, 64, 32, 16))
    return pl.pallas_call(
        functools.partial(_router_kernel, n_exp=n_exp),
        grid=(m // tm,),
        in_specs=[pl.BlockSpec((tm, d), lambda i: (i, 0)), pl.BlockSpec((1, d), lambda i: (0, 0)),
                  pl.BlockSpec((d, LANES), lambda i: (0, 0))],
        out_specs=(pl.BlockSpec((tm, d), lambda i: (i, 0)), pl.BlockSpec((tm, LANES), lambda i: (i, 0))),
        out_shape=(jax.ShapeDtypeStruct((m, d), f32), jax.ShapeDtypeStruct((m, LANES), f32)),
        compiler_params=_cp("parallel"),
    )(x, g.reshape(1, d).astype(f32), wr)


MOE_ROW_TILE = 256


def _dispatch_plan(expert_ids, n_exp):
    m, k = expert_ids.shape
    tile = MOE_ROW_TILE
    n_rows = -(-(m * k) // tile) * tile + n_exp * tile
    flat = expert_ids.reshape(m * k)
    onehot = (flat[:, None] == jnp.arange(n_exp, dtype=jnp.int32)[None, :]).astype(jnp.int32)
    rank = jnp.sum((jnp.cumsum(onehot, axis=0) - onehot) * onehot, axis=1)
    padded = -(-jnp.sum(onehot, axis=0) // tile) * tile
    ends = jnp.cumsum(padded)
    pos = (ends - padded)[flat] + rank
    n_tiles = n_rows // tile
    tile_start = jnp.arange(n_tiles, dtype=jnp.int32) * tile
    tile_expert = jnp.minimum(jnp.sum((tile_start[:, None] >= ends[None, :]).astype(jnp.int32), axis=1), n_exp - 1)
    n_used = (ends[-1] // tile).astype(jnp.int32).reshape(1)
    src_tok = jnp.zeros((n_rows,), jnp.int32).at[pos].set(jnp.arange(m * k, dtype=jnp.int32) // k)
    return pos.astype(jnp.int32), src_tok, tile_expert.astype(jnp.int32), n_used, n_rows


def _gather_rows_kernel(src_ref, nused_ref, h_hbm, o_ref, buf, sem, *, tile):
    r = pl.program_id(0)

    @pl.when(r < nused_ref[0])
    def _():
        def start(i, _):
            pltpu.make_async_copy(h_hbm.at[pl.ds(src_ref[r * tile + i], 1), :], buf.at[pl.ds(i, 1), :], sem.at[0]).start()
            return 0
        lax.fori_loop(0, tile, start, 0)

        def wait(i, _):
            pltpu.make_async_copy(h_hbm.at[pl.ds(0, 1), :], buf.at[pl.ds(i, 1), :], sem.at[0]).wait()
            return 0
        lax.fori_loop(0, tile, wait, 0)
        o_ref[...] = buf[...].astype(o_ref.dtype)

    @pl.when(r >= nused_ref[0])
    def _():
        o_ref[...] = jnp.zeros_like(o_ref)


def gather_rows(h, src_tok, n_used, n_rows):
    m, d = h.shape
    tile = MOE_ROW_TILE
    return pl.pallas_call(
        functools.partial(_gather_rows_kernel, tile=tile),
        grid_spec=pltpu.PrefetchScalarGridSpec(
            num_scalar_prefetch=2, grid=(n_rows // tile,),
            in_specs=[pl.BlockSpec(memory_space=pl.ANY)],
            out_specs=pl.BlockSpec((tile, d), lambda r, src, nu: (r, 0)),
            scratch_shapes=[pltpu.VMEM((tile, d), f32), pltpu.SemaphoreType.DMA((1,))]),
        out_shape=jax.ShapeDtypeStruct((n_rows, d), bf16),
        compiler_params=_cp("arbitrary"),
    )(src_tok, n_used, h)


def _new_expert_tile(te_ref, nused_ref):
    r = pl.program_id(1)
    changed = jnp.logical_or(r == 0, te_ref[r] != te_ref[jnp.maximum(r - 1, 0)])
    return jnp.logical_and(changed, r < nused_ref[0])


def _moe_gateup_kernel(te_ref, nused_ref, h_ref, wg_ref, wu_ref, o_ref, wg_sc, wu_sc):
    @pl.when(_new_expert_tile(te_ref, nused_ref))
    def _():
        wg_sc[...] = wg_ref[...].astype(bf16)
        wu_sc[...] = wu_ref[...].astype(bf16)

    @pl.when(pl.program_id(1) < nused_ref[0])
    def _():
        h = h_ref[...]
        o_ref[...] = _swiglu(_dot(h, wg_sc[...]), _dot(h, wu_sc[...])).astype(o_ref.dtype)

    @pl.when(pl.program_id(1) >= nused_ref[0])
    def _():
        o_ref[...] = jnp.zeros_like(o_ref)


def moe_gateup(hs, wg, wu, tile_expert, n_used):
    n_rows, d = hs.shape
    f = wg.shape[2]
    tile = MOE_ROW_TILE
    tn = _pick(f, (512, 256, 128))
    row = lambda j, r, te, nu: (jnp.minimum(r, nu[0] - 1), 0)
    wmap = lambda j, r, te, nu: (te[jnp.minimum(r, nu[0] - 1)], 0, j)
    return pl.pallas_call(
        _moe_gateup_kernel,
        grid_spec=pltpu.PrefetchScalarGridSpec(
            num_scalar_prefetch=2, grid=(f // tn, n_rows // tile),
            in_specs=[pl.BlockSpec((tile, d), row), pl.BlockSpec((None, d, tn), wmap), pl.BlockSpec((None, d, tn), wmap)],
            out_specs=pl.BlockSpec((tile, tn), lambda j, r, te, nu: (r, j)),
            scratch_shapes=[pltpu.VMEM((d, tn), bf16), pltpu.VMEM((d, tn), bf16)]),
        out_shape=jax.ShapeDtypeStruct((n_rows, f), bf16),
        compiler_params=_cp("arbitrary", "arbitrary"),
    )(tile_expert, n_used, hs, wg, wu)


def _moe_down_kernel(te_ref, nused_ref, a_ref, wd_ref, o_ref, wd_sc):
    @pl.when(_new_expert_tile(te_ref, nused_ref))
    def _():
        wd_sc[...] = wd_ref[...].astype(bf16)

    @pl.when(pl.program_id(1) < nused_ref[0])
    def _():
        o_ref[...] = _dot(a_ref[...], wd_sc[...])

    @pl.when(pl.program_id(1) >= nused_ref[0])
    def _():
        o_ref[...] = jnp.zeros_like(o_ref)


def moe_down(a, wd, tile_expert, n_used):
    n_rows, f = a.shape
    d = wd.shape[2]
    tile = MOE_ROW_TILE
    tn = _pick(d, (1024, 512, 256, 128))
    return pl.pallas_call(
        _moe_down_kernel,
        grid_spec=pltpu.PrefetchScalarGridSpec(
            num_scalar_prefetch=2, grid=(d // tn, n_rows // tile),
            in_specs=[pl.BlockSpec((tile, f), lambda j, r, te, nu: (jnp.minimum(r, nu[0] - 1), 0)),
                      pl.BlockSpec((None, f, tn), lambda j, r, te, nu: (te[jnp.minimum(r, nu[0] - 1)], 0, j))],
            out_specs=pl.BlockSpec((tile, tn), lambda j, r, te, nu: (r, j)),
            scratch_shapes=[pltpu.VMEM((f, tn), bf16)]),
        out_shape=jax.ShapeDtypeStruct((n_rows, d), f32),
        compiler_params=_cp("arbitrary", "arbitrary"),
    )(tile_expert, n_used, a, wd)


def _moe_combine_kernel(pos_ref, x_ref, w_ref, y_hbm, o_ref, buf, sem, *, tile, k):
    i0 = pl.program_id(0) * tile

    def start(i, _):
        for kk in range(k):
            pltpu.make_async_copy(y_hbm.at[pl.ds(pos_ref[(i0 + i) * k + kk], 1), :], buf.at[kk, pl.ds(i, 1), :],
                                  sem.at[kk]).start()
        return 0
    lax.fori_loop(0, tile, start, 0)

    def wait(i, _):
        for kk in range(k):
            pltpu.make_async_copy(y_hbm.at[pl.ds(0, 1), :], buf.at[kk, pl.ds(i, 1), :], sem.at[kk]).wait()
        return 0
    lax.fori_loop(0, tile, wait, 0)
    out = x_ref[...]
    for kk in range(k):
        out = out + w_ref[:, kk:kk + 1] * buf[kk]
    o_ref[...] = out


def moe_combine(x, y, pos, weights):
    m, d = x.shape
    k = weights.shape[1]
    tile = _pick(m, (128, 64, 32, 16, 8))
    return pl.pallas_call(
        functools.partial(_moe_combine_kernel, tile=tile, k=k),
        grid_spec=pltpu.PrefetchScalarGridSpec(
            num_scalar_prefetch=1, grid=(m // tile,),
            in_specs=[pl.BlockSpec((tile, d), lambda i, p: (i, 0)), pl.BlockSpec((tile, k), lambda i, p: (i, 0)),
                      pl.BlockSpec(memory_space=pl.ANY)],
            out_specs=pl.BlockSpec((tile, d), lambda i, p: (i, 0)),
            scratch_shapes=[pltpu.VMEM((k, tile, d), f32), pltpu.SemaphoreType.DMA((k,))]),
        out_shape=jax.ShapeDtypeStruct((m, d), f32),
        compiler_params=_cp("arbitrary"),
    )(pos, x, weights, y)


def _upper(tk):
    j = lax.broadcasted_iota(jnp.int32, (tk, tk), 0)
    s = lax.broadcasted_iota(jnp.int32, (tk, tk), 1)
    return jnp.where(j > s, 1.0, 0.0).astype(bf16)


def _sb_block(q, k, v, causal, carry, acc):
    rows = q.shape[0]
    tk = k.shape[0]
    z = _dot_t(q, k)
    sp = jnp.maximum(z, 0.0) + jnp.log1p(jnp.exp(-jnp.abs(z)))
    l1 = jnp.where(causal, -sp, 0.0)
    hi, lo = _split(l1)
    cum = _dot(jnp.concatenate([hi, lo], axis=0), _upper(tk))
    later = carry + (cum[:rows] + cum[rows:])
    w = jnp.where(causal, jnp.exp(z - sp + later), 0.0)
    acc = acc + _dot(w.astype(bf16), v)
    carry = carry + jnp.sum(l1, axis=1, keepdims=True)
    return carry, acc


def _sb_past_blocks(q, kt, vt, carry, acc, nblk, tk):
    rows = q.shape[0]
    z = _dot(q, kt)
    zs = jnp.concatenate([z[:, b * tk:(b + 1) * tk] for b in range(nblk)], axis=0)
    sp = jnp.maximum(zs, 0.0) + jnp.log1p(jnp.exp(-jnp.abs(zs)))
    l1 = -sp
    hi, lo = _split(l1)
    cum = _dot(jnp.concatenate([hi, lo], axis=0), _upper(tk))
    cum = cum[:nblk * rows] + cum[nblk * rows:]
    tot = jnp.sum(l1, axis=1, keepdims=True)
    carries = [None] * nblk
    for b in reversed(range(nblk)):
        carries[b] = carry
        carry = carry + tot[b * rows:(b + 1) * rows]
    w = jnp.exp(zs - sp + cum + jnp.concatenate(carries, axis=0))
    wl = jnp.concatenate([w[b * rows:(b + 1) * rows] for b in range(nblk)], axis=1).astype(bf16)
    return carry, acc + _dot_t(wl, vt)


def _sb_prompt_kernel(q_ref, k_ref, v_ref, o_ref, carry_sc, acc_sc, *, H, tq, tk):
    qi = pl.program_id(1)
    st = pl.program_id(2)
    dh = q_ref.shape[2]
    rows = H * tq
    kb = (qi * tq + tq - 1) // tk - st

    @pl.when(st == 0)
    def _():
        carry_sc[...] = jnp.zeros_like(carry_sc)
        acc_sc[...] = jnp.zeros_like(acc_sc)

    @pl.when(kb >= 0)
    def _():
        q = q_ref[...].reshape(rows, dh)
        qpos = qi * tq + lax.broadcasted_iota(jnp.int32, (rows, tk), 0) % tq
        kpos = kb * tk + lax.broadcasted_iota(jnp.int32, (rows, tk), 1)
        carry, acc = _sb_block(q, k_ref[...].astype(bf16), v_ref[...].astype(bf16), kpos < qpos,
                               carry_sc[...], acc_sc[...])
        carry_sc[...] = carry
        acc_sc[...] = acc

    @pl.when(st == pl.num_programs(2) - 1)
    def _():
        for h in range(H):
            o_ref[:, h * dh:(h + 1) * dh] = acc_sc[h * tq:(h + 1) * tq, :]


def sb_prompt(q, k, v, B, T):
    H, _, dh = q.shape
    tq = _pick(T, (128, 64, 32, 16))
    tk = _pick(T, (256, 128, 64, 32, 16))
    nq, nk = T // tq, T // tk

    def kv_map(b, qi, st):
        return (b * nk + jnp.maximum((qi * tq + tq - 1) // tk - st, 0), 0)

    return pl.pallas_call(
        functools.partial(_sb_prompt_kernel, H=H, tq=tq, tk=tk),
        grid=(B, nq, nk),
        in_specs=[pl.BlockSpec((H, tq, dh), lambda b, qi, st: (0, b * nq + qi, 0)),
                  pl.BlockSpec((tk, dh), kv_map), pl.BlockSpec((tk, dh), kv_map)],
        out_specs=pl.BlockSpec((tq, H * dh), lambda b, qi, st: (b * nq + qi, 0)),
        out_shape=jax.ShapeDtypeStruct((B * T, H * dh), f32),
        scratch_shapes=[pltpu.VMEM((H * tq, 1), f32), pltpu.VMEM((H * tq, dh), f32)],
        compiler_params=_cp("parallel", "parallel", "arbitrary"),
    )(q, k, v)


def _page_dst(buf, slot, j, page, transposed):
    if transposed:
        return buf.at[slot, :, pl.ds(j * page, page)]
    return buf.at[slot, pl.ds(j * page, page), :]


def _paged_start(pt_ref, seq, first_page, n_pages, page, layer, pairs, slot):
    def body(j, _):
        pg = pt_ref[seq, first_page + j]
        for cache, buf, sem, tr in pairs:
            pltpu.make_async_copy(cache.at[layer, pg], _page_dst(buf, slot, j, page, tr), sem.at[slot]).start()
        return 0
    lax.fori_loop(0, n_pages, body, 0)


def _paged_wait(n_pages, page, layer, pairs, slot):
    def body(j, _):
        for cache, buf, sem, tr in pairs:
            pltpu.make_async_copy(cache.at[layer, 0], _page_dst(buf, slot, j, page, tr), sem.at[slot]).wait()
        return 0
    lax.fori_loop(0, n_pages, body, 0)


def _sb_sample_kernel(pt_ref, q_ref, kn_ref, vn_ref, kc_hbm, vc_hbm, o_ref, kbuf, vbuf, ksem, vsem,
                      *, layer, n_pages, page, DT, tk):
    b = pl.program_id(0)
    nb = pl.num_programs(0)
    slot = b % 2
    pairs = [(kc_hbm, kbuf, ksem, True), (vc_hbm, vbuf, vsem, True)]

    @pl.when(b == 0)
    def _():
        _paged_start(pt_ref, 0, 0, n_pages, page, layer, pairs, 0)

    @pl.when(b + 1 < nb)
    def _():
        _paged_start(pt_ref, b + 1, 0, n_pages, page, layer, pairs, 1 - slot)

    _paged_wait(n_pages, page, layer, pairs, slot)

    q = q_ref[0]
    rows, dh = q.shape
    t_idx = lax.broadcasted_iota(jnp.int32, (rows, NEW_PAD), 0) % DT
    j_idx = lax.broadcasted_iota(jnp.int32, (rows, NEW_PAD), 1)
    carry, acc = _sb_block(q, kn_ref[0].astype(bf16), vn_ref[0].astype(bf16), j_idx < t_idx,
                           jnp.zeros((rows, 1), f32), jnp.zeros((rows, dh), f32))
    n_grp = (n_pages * page) // tk
    per = _pick(n_grp, (16, 8, 4, 2, 1))
    for sb in reversed(range(n_grp // per)):
        lo, hi = sb * per * tk, (sb + 1) * per * tk
        carry, acc = _sb_past_blocks(q, kbuf[slot, :, lo:hi].astype(bf16), vbuf[slot, :, lo:hi].astype(bf16),
                                     carry, acc, per, tk)
    o_ref[0] = acc


def _sample_call(kernel, pt, blocked, caches, out_shape, out_block, scratch, grid, n_prefetch=1):
    nd = len(grid)
    def imap(shape):
        if nd == 1:
            return lambda b, pt_: (b,) + (0,) * (len(shape) - 1)
        return lambda b, c, pt_: (b,) + (0,) * (len(shape) - 1)
    in_specs = [pl.BlockSpec((1,) + a.shape[1:], imap(a.shape)) for a in blocked]
    in_specs += [pl.BlockSpec(memory_space=pl.ANY) for _ in caches]
    return pl.pallas_call(
        kernel,
        grid_spec=pltpu.PrefetchScalarGridSpec(
            num_scalar_prefetch=n_prefetch, grid=grid, in_specs=in_specs,
            out_specs=pl.BlockSpec(out_block, imap(out_shape.shape)), scratch_shapes=scratch),
        out_shape=out_shape,
        compiler_params=_cp(*(("arbitrary",) * nd)),
    )(pt, *blocked, *caches)


def sb_sample(q, k_new, v_new, cache_k, cache_v, page_table, layer, DT):
    DB, rows, dh = q.shape
    n_pages = page_table.shape[1]
    page = cache_k.shape[3]
    tk = 2 * page
    assert n_pages % 2 == 0
    P = n_pages * page
    scratch = [pltpu.VMEM((2, dh, P), f32), pltpu.VMEM((2, dh, P), f32),
               pltpu.SemaphoreType.DMA((2,)), pltpu.SemaphoreType.DMA((2,))]
    return _sample_call(
        functools.partial(_sb_sample_kernel, layer=layer, n_pages=n_pages, page=page, DT=DT, tk=tk),
        page_table, [q, k_new, v_new], [cache_k, cache_v],
        jax.ShapeDtypeStruct((DB, rows, dh), f32), (1, rows, dh), scratch, (DB,))


def _softmax_step(s, v, m_prev, l_prev, acc_prev, mask=None, v_transposed=False):
    if mask is not None:
        s = jnp.where(mask, s, NEG)
    m_new = jnp.maximum(m_prev, jnp.max(s, axis=1, keepdims=True))
    alpha = jnp.exp(m_prev - m_new)
    p = jnp.exp(s - m_new)
    if mask is not None:
        p = jnp.where(mask, p, 0.0)
    l_new = alpha * l_prev + jnp.sum(p, axis=1, keepdims=True)
    pb = p.astype(bf16)
    acc_new = alpha * acc_prev + (_dot_t(pb, v) if v_transposed else _dot(pb, v))
    return m_new, l_new, acc_new


def _mla_prompt_kernel(ql_ref, qp_ref, ckv_ref, kpe_ref, o_ref, m_sc, l_sc, acc_sc, *, H, tq, tk, scale):
    qi = pl.program_id(1)
    kj = pl.program_id(2)
    rows = H * tq
    last = (qi * tq + tq - 1) // tk

    @pl.when(kj == 0)
    def _():
        m_sc[...] = jnp.full_like(m_sc, NEG)
        l_sc[...] = jnp.zeros_like(l_sc)
        acc_sc[...] = jnp.zeros_like(acc_sc)

    @pl.when(kj <= last)
    def _():
        ck = ckv_ref[...].astype(bf16)
        kp = kpe_ref[...].astype(bf16)
        hg = _pick(H, (HEAD_GROUP, 2, 1))
        gr = hg * tq
        qpos = qi * tq + lax.broadcasted_iota(jnp.int32, (gr, tk), 0) % tq
        kpos = kj * tk + lax.broadcasted_iota(jnp.int32, (gr, tk), 1)
        mask = kpos <= qpos
        for g in range(H // hg):
            rs = slice(g * gr, (g + 1) * gr)
            ql = ql_ref[g * hg:(g + 1) * hg].reshape(gr, ql_ref.shape[2])
            qp = qp_ref[g * hg:(g + 1) * hg].reshape(gr, qp_ref.shape[2])
            s = (_dot_t(ql, ck) + _dot_t(qp, kp)) * scale
            m, l, acc = _softmax_step(s, ck, m_sc[rs], l_sc[rs], acc_sc[rs], mask)
            m_sc[rs] = m
            l_sc[rs] = l
            acc_sc[rs] = acc

    @pl.when(kj == pl.num_programs(2) - 1)
    def _():
        o_ref[...] = (acc_sc[...] / l_sc[...]).reshape(o_ref.shape).astype(o_ref.dtype)


def mla_prompt(q_lat, q_pe, ckv, kpe, B, T, scale):
    H, _, C = q_lat.shape
    R = q_pe.shape[2]
    tq = _pick(T, (128, 64, 32, 16))
    tk = _pick(T, (512, 256, 128, 64, 32, 16))
    nq, nk = T // tq, T // tk

    def kv_map(b, qi, kj):
        return (b * nk + jnp.minimum(kj, (qi * tq + tq - 1) // tk), 0)

    q_map = lambda b, qi, kj: (0, b * nq + qi, 0)
    return pl.pallas_call(
        functools.partial(_mla_prompt_kernel, H=H, tq=tq, tk=tk, scale=scale),
        grid=(B, nq, nk),
        in_specs=[pl.BlockSpec((H, tq, C), q_map), pl.BlockSpec((H, tq, R), q_map),
                  pl.BlockSpec((tk, C), kv_map), pl.BlockSpec((tk, R), kv_map)],
        out_specs=pl.BlockSpec((H, tq, C), q_map),
        out_shape=jax.ShapeDtypeStruct((H, B * T, C), bf16),
        scratch_shapes=[pltpu.VMEM((H * tq, 1), f32), pltpu.VMEM((H * tq, 1), f32), pltpu.VMEM((H * tq, C), f32)],
        compiler_params=_cp("parallel", "parallel", "arbitrary"),
    )(q_lat, q_pe, ckv, kpe)


def _mla_sample_kernel(pt_ref, ql_ref, qp_ref, cn_ref, pn_ref, ckv_hbm, kpe_hbm, o_ref,
                       cbuf, pbuf, csem, psem, m_sc, l_sc, acc_sc, *, layer, chunk, page, DT, scale):
    b = pl.program_id(0)
    c = pl.program_id(1)
    nc = pl.num_programs(1)
    step = b * nc + c
    total = pl.num_programs(0) * nc
    slot = step % 2
    pairs = [(ckv_hbm, cbuf, csem, False), (kpe_hbm, pbuf, psem, True)]

    @pl.when(step == 0)
    def _():
        _paged_start(pt_ref, 0, 0, chunk, page, layer, pairs, 0)

    @pl.when(step + 1 < total)
    def _():
        nxt = step + 1
        _paged_start(pt_ref, nxt // nc, (nxt % nc) * chunk, chunk, page, layer, pairs, 1 - slot)

    _paged_wait(chunk, page, layer, pairs, slot)

    ql = ql_ref[0]
    qp = qp_ref[0]
    rows = ql.shape[0]

    @pl.when(c == 0)
    def _():
        cn = cn_ref[0].astype(bf16)
        pn = pn_ref[0].astype(bf16)
        s = (_dot_t(ql, cn) + _dot_t(qp, pn)) * scale
        t_idx = lax.broadcasted_iota(jnp.int32, (rows, NEW_PAD), 0) % DT
        j_idx = lax.broadcasted_iota(jnp.int32, (rows, NEW_PAD), 1)
        m, l, acc = _softmax_step(s, cn, jnp.full((rows, 1), NEG, f32), jnp.zeros((rows, 1), f32),
                                  jnp.zeros(acc_sc.shape, f32), j_idx <= t_idx)
        m_sc[...] = m
        l_sc[...] = l
        acc_sc[...] = acc

    width = (chunk * page) // MLA_SAMPLE_PARTS
    states = [(m_sc[...], l_sc[...], acc_sc[...])]
    for i in range(MLA_SAMPLE_PARTS):
        ck = cbuf[slot, i * width:(i + 1) * width, :].astype(bf16)
        kpt = pbuf[slot, :, i * width:(i + 1) * width].astype(bf16)
        s = (_dot_t(ql, ck) + _dot(qp, kpt)) * scale
        m_i = jnp.max(s, axis=1, keepdims=True)
        p = jnp.exp(s - m_i)
        states.append((m_i, jnp.sum(p, axis=1, keepdims=True), _dot(p.astype(bf16), ck)))
    m = states[0][0]
    for st in states[1:]:
        m = jnp.maximum(m, st[0])
    l = jnp.zeros_like(m)
    acc = jnp.zeros(acc_sc.shape, f32)
    for m_i, l_i, acc_i in states:
        w_i = jnp.exp(m_i - m)
        l = l + w_i * l_i
        acc = acc + w_i * acc_i
    m_sc[...] = m
    l_sc[...] = l
    acc_sc[...] = acc

    @pl.when(c == nc - 1)
    def _():
        o_ref[0] = (acc_sc[...] / l_sc[...]).astype(o_ref.dtype)


def mla_sample(q_lat, q_pe, ckv_new, kpe_new, cache_ckv, cache_kpe, page_table, layer, DT, scale):
    DB, rows, C = q_lat.shape
    R = q_pe.shape[2]
    n_pages = page_table.shape[1]
    page = cache_ckv.shape[2]
    chunk = _pick(n_pages, (32, 16, 8, 4, 2, 1))
    scratch = [pltpu.VMEM((2, chunk * page, C), f32), pltpu.VMEM((2, R, chunk * page), f32),
               pltpu.SemaphoreType.DMA((2,)), pltpu.SemaphoreType.DMA((2,)),
               pltpu.VMEM((rows, 1), f32), pltpu.VMEM((rows, 1), f32), pltpu.VMEM((rows, C), f32)]
    return _sample_call(
        functools.partial(_mla_sample_kernel, layer=layer, chunk=chunk, page=page, DT=DT, scale=scale),
        page_table, [q_lat, q_pe, ckv_new, kpe_new], [cache_ckv, cache_kpe],
        jax.ShapeDtypeStruct((DB, rows, C), bf16), (1, rows, C), scratch, (DB, n_pages // chunk))


def _moba_select(q, kmean, n_valid):
    nb, dh = kmean.shape
    km = jnp.concatenate([kmean, jnp.zeros((LANES - nb, dh), f32)], axis=0) if nb < LANES else kmean
    k_hi, k_lo = _split(km)
    return _moba_topk(_dot_t(q, k_hi) + _dot_t(q, k_lo), n_valid)


def _moba_topk(gate, n_valid):
    blk = lax.broadcasted_iota(jnp.int32, gate.shape, 1)
    valid = blk < n_valid
    work = jnp.where(valid, gate, -jnp.inf)
    sel = jnp.zeros(gate.shape, jnp.bool_)
    for _ in range(MB_TOPK):
        mx = jnp.max(work, axis=1, keepdims=True)
        ix = jnp.min(jnp.where(work == mx, blk, LANES), axis=1, keepdims=True)
        pick = blk == ix
        sel = jnp.logical_or(sel, pick)
        work = jnp.where(pick, -jnp.inf, work)
    return jnp.logical_and(sel, valid)


def _block_means(k, nb):
    return jnp.mean(k.reshape(nb, MB_BLOCK, k.shape[1]), axis=1)


def _moba_prompt_kernel(q_ref, k_ref, v_ref, slope_ref, o_ref, m_sc, l_sc, acc_sc, *, H, tq, nb):
    qi = pl.program_id(1)
    dh = q_ref.shape[2]
    rows = H * tq
    q = q_ref[...].reshape(rows, dh)
    q_blk = (qi * tq) // MB_BLOCK
    sel = _moba_select(q, _block_means(k_ref[...], nb), q_blk)
    sel_f = jnp.where(sel, 1.0, 0.0)
    blk = lax.broadcasted_iota(jnp.int32, sel.shape, 1)
    slope = slope_ref[...]
    qpos = qi * tq + lax.broadcasted_iota(jnp.int32, (rows, MB_BLOCK), 0) % tq
    m_sc[...] = jnp.full_like(m_sc, NEG)
    l_sc[...] = jnp.zeros_like(l_sc)
    acc_sc[...] = jnp.zeros_like(acc_sc)
    for kb in range(nb):
        @pl.when(kb <= q_blk)
        def _(kb=kb):
            kk = k_ref[kb * MB_BLOCK:(kb + 1) * MB_BLOCK, :].astype(bf16)
            vv = v_ref[kb * MB_BLOCK:(kb + 1) * MB_BLOCK, :].astype(bf16)
            kpos = kb * MB_BLOCK + lax.broadcasted_iota(jnp.int32, (rows, MB_BLOCK), 1)
            s = _dot_t(q, kk) - slope * (qpos - kpos).astype(f32)
            chosen = jnp.sum(jnp.where(blk == kb, sel_f, 0.0), axis=1, keepdims=True)
            causal = jnp.where(kpos <= qpos, 1.0, 0.0)
            mask = jnp.where(kb == q_blk, causal, chosen) > 0.5
            m, l, acc = _softmax_step(s, vv, m_sc[...], l_sc[...], acc_sc[...], mask)
            m_sc[...] = m
            l_sc[...] = l
            acc_sc[...] = acc
    out = acc_sc[...] / l_sc[...]
    for h in range(H):
        o_ref[:, h * dh:(h + 1) * dh] = out[h * tq:(h + 1) * tq, :]


def moba_prompt(q, k, v, slopes, B, T):
    H, _, dh = q.shape
    tq = _pick(T, (128, 64, 32, 16))
    assert MB_BLOCK % tq == 0 and T % MB_BLOCK == 0
    nq, nb = T // tq, T // MB_BLOCK
    slope_rows = jnp.repeat(slopes.astype(f32), tq).reshape(H * tq, 1)
    return pl.pallas_call(
        functools.partial(_moba_prompt_kernel, H=H, tq=tq, nb=nb),
        grid=(B, nq),
        in_specs=[pl.BlockSpec((H, tq, dh), lambda b, qi: (0, b * nq + qi, 0)),
                  pl.BlockSpec((T, dh), lambda b, qi: (b, 0)), pl.BlockSpec((T, dh), lambda b, qi: (b, 0)),
                  pl.BlockSpec((H * tq, 1), lambda b, qi: (0, 0))],
        out_specs=pl.BlockSpec((tq, H * dh), lambda b, qi: (b * nq + qi, 0)),
        out_shape=jax.ShapeDtypeStruct((B * T, H * dh), f32),
        scratch_shapes=[pltpu.VMEM((H * tq, 1), f32), pltpu.VMEM((H * tq, 1), f32), pltpu.VMEM((H * tq, dh), f32)],
        compiler_params=_cp("parallel", "arbitrary"),
    )(q, k, v, slope_rows)


def _moba_sample_kernel(pt_ref, q_ref, kn_ref, vn_ref, slope_ref, kc_hbm, vc_hbm, o_ref, kbuf, vbuf, ksem, vsem,
                        *, layer, n_pages, page, DT):
    b = pl.program_id(0)
    nbatch = pl.num_programs(0)
    slot = b % 2
    pairs = [(kc_hbm, kbuf, ksem, True), (vc_hbm, vbuf, vsem, True)]

    @pl.when(b == 0)
    def _():
        _paged_start(pt_ref, 0, 0, n_pages, page, layer, pairs, 0)

    @pl.when(b + 1 < nbatch)
    def _():
        _paged_start(pt_ref, b + 1, 0, n_pages, page, layer, pairs, 1 - slot)

    _paged_wait(n_pages, page, layer, pairs, slot)

    q = q_ref[0]
    rows, dh = q.shape
    P = n_pages * page
    nb = P // MB_BLOCK
    col_id = lax.broadcasted_iota(jnp.int32, (dh, LANES), 1)
    kmt = jnp.zeros((dh, LANES), f32)
    for kb in range(nb):
        col = jnp.sum(kbuf[slot, :, kb * MB_BLOCK:(kb + 1) * MB_BLOCK], axis=1, keepdims=True) * (1.0 / MB_BLOCK)
        kmt = jnp.where(col_id == kb, col, kmt)
    km_hi, km_lo = _split(kmt)
    sel = _moba_topk(_dot(q, km_hi) + _dot(q, km_lo), nb)
    sel_f = jnp.where(sel, 1.0, 0.0)
    blk = lax.broadcasted_iota(jnp.int32, sel.shape, 1)
    slope = slope_ref[...]
    per = _pick(nb, (16, 8, 4, 2, 1))
    width = per * MB_BLOCK
    t_idx = lax.broadcasted_iota(jnp.int32, (rows, width), 0) % DT
    lane = lax.broadcasted_iota(jnp.int32, (rows, width), 1)
    rel = (t_idx - lane).astype(f32)
    m = jnp.full((rows, 1), NEG, f32)
    l = jnp.zeros((rows, 1), f32)
    acc = jnp.zeros((rows, dh), f32)
    for sb in range(nb // per):
        lo = sb * width
        kt = kbuf[slot, :, lo:lo + width].astype(bf16)
        vt = vbuf[slot, :, lo:lo + width].astype(bf16)
        s = _dot(q, kt) - slope * (rel + float(P - lo))
        chosen = [jnp.sum(jnp.where(blk == sb * per + b, sel_f, 0.0), axis=1, keepdims=True) for b in range(per)]
        mask = jnp.concatenate([jnp.broadcast_to(c, (rows, MB_BLOCK)) for c in chosen], axis=1) > 0.5
        m, l, acc = _softmax_step(s, vt, m, l, acc, mask, v_transposed=True)
    t_new = lax.broadcasted_iota(jnp.int32, (rows, NEW_PAD), 0) % DT
    j_new = lax.broadcasted_iota(jnp.int32, (rows, NEW_PAD), 1)
    s = _dot_t(q, kn_ref[0].astype(bf16)) - slope * (t_new - j_new).astype(f32)
    m, l, acc = _softmax_step(s, vn_ref[0].astype(bf16), m, l, acc, j_new <= t_new)
    o_ref[0] = acc / l


def moba_sample(q, k_new, v_new, slopes, cache_k, cache_v, page_table, layer, DT):
    DB, rows, dh = q.shape
    H = rows // DT
    n_pages = page_table.shape[1]
    page = cache_k.shape[3]
    P = n_pages * page
    assert P % MB_BLOCK == 0 and DT < MB_BLOCK and P // MB_BLOCK <= LANES
    slope_rows = jnp.repeat(slopes.astype(f32), DT).reshape(H * DT, 1)
    scratch = [pltpu.VMEM((2, dh, P), f32), pltpu.VMEM((2, dh, P), f32),
               pltpu.SemaphoreType.DMA((2,)), pltpu.SemaphoreType.DMA((2,))]
    nd_map = lambda b, pt_: (0, 0)
    in_specs = [pl.BlockSpec((1, rows, dh), lambda b, pt_: (b, 0, 0)),
                pl.BlockSpec((1, NEW_PAD, dh), lambda b, pt_: (b, 0, 0)),
                pl.BlockSpec((1, NEW_PAD, dh), lambda b, pt_: (b, 0, 0)),
                pl.BlockSpec((rows, 1), nd_map),
                pl.BlockSpec(memory_space=pl.ANY), pl.BlockSpec(memory_space=pl.ANY)]
    return pl.pallas_call(
        functools.partial(_moba_sample_kernel, layer=layer, n_pages=n_pages, page=page, DT=DT),
        grid_spec=pltpu.PrefetchScalarGridSpec(
            num_scalar_prefetch=1, grid=(DB,), in_specs=in_specs,
            out_specs=pl.BlockSpec((1, rows, dh), lambda b, pt_: (b, 0, 0)), scratch_shapes=scratch),
        out_shape=jax.ShapeDtypeStruct((DB, rows, dh), f32),
        compiler_params=_cp("arbitrary"),
    )(page_table, q, k_new, v_new, slope_rows, cache_k, cache_v)


def _rope_tables(pos, rope):
    inv = ROPE_THETA ** (-jnp.arange(0, rope, 2, dtype=f32) / rope)
    ang = pos.astype(f32)[:, None] * inv[None, :]
    cos, sin = jnp.cos(ang), jnp.sin(ang)
    return jnp.concatenate([cos, cos], axis=-1), jnp.concatenate([-sin, sin], axis=-1)


def kernel(x_prompt, x_sample, cache_sb_k, cache_sb_v, cache_mla_ckv, cache_mla_kpe, cache_moba_k, cache_moba_v,
           page_table, ln_mix, w_in, q_a_norm, w_q_b, kv_a_norm, w_kv_b, g_mix, w_out, ln_ff,
           w_ff_gate, w_ff_up, w_ff_down, w_router, w_ex_gate, w_ex_up, w_ex_down, ln_final):
    B, T, D = x_prompt.shape
    DB, DT, _ = x_sample.shape
    depth = w_in.shape[0]
    in_width = w_in.shape[2]
    H = w_q_b.shape[2]
    sb_dh = cache_sb_k.shape[3]
    mb_dh = cache_moba_k.shape[3]
    kvl = cache_mla_ckv.shape[3]
    rope = cache_mla_kpe.shape[3]
    ql = q_a_norm.shape[1]
    nope = w_q_b.shape[3] - rope
    vdim = w_kv_b.shape[3] - nope
    dims = dict(H=H, sb_dh=sb_dh, mb_dh=mb_dh, ql=ql, kvl=kvl, rope=rope, nope=nope)
    assert in_width == H * sb_dh + 2 * sb_dh + ql + kvl + rope + H * mb_dh + 2 * mb_dh
    n_pages = page_table.shape[1]
    page = cache_sb_k.shape[2]
    P = n_pages * page
    Mp, Ms = B * T, DB * DT
    mla_scale = (nope + rope) ** -0.5
    slopes = 2.0 ** (-8.0 * jnp.arange(1, H + 1, dtype=f32) / H)

    pos = jnp.concatenate([jnp.tile(jnp.arange(T, dtype=jnp.int32), B),
                           jnp.tile(P + jnp.arange(DT, dtype=jnp.int32), DB)])
    cos64, sin64 = _rope_tables(pos, rope)
    in_pad = (-in_width) % 512

    def to_seq(a):
        w = a.shape[-1]
        return a.reshape(H, DB, DT, w).transpose(1, 0, 2, 3).reshape(DB, H * DT, w)

    def from_seq_heads(a):
        w = a.shape[-1]
        return a.reshape(DB, H, DT, w).transpose(1, 0, 2, 3).reshape(H, Ms, w)

    def from_seq_rows(a):
        w = a.shape[-1]
        return a.reshape(DB, H, DT, w).transpose(0, 2, 1, 3).reshape(Ms, H * w)

    def new_pad(a):
        return jnp.pad(a.reshape(DB, DT, a.shape[-1]), ((0, 0), (0, NEW_PAD - DT), (0, 0)))

    sbk_t, sbv_t, kpe_t, mbk_t, mbv_t = (jnp.swapaxes(c, 2, 3) for c in
                                         (cache_sb_k, cache_sb_v, cache_mla_kpe, cache_moba_k, cache_moba_v))

    x = jnp.concatenate([x_prompt.reshape(Mp, D), x_sample.reshape(Ms, D)], axis=0)
    rows_p = [[] for _ in range(6)]
    rows_s = [[] for _ in range(6)]
    for l in range(depth):
        w_in_l = jnp.pad(w_in[l], ((0, 0), (0, in_pad))).astype(bf16)
        w_qn = w_q_b[l][:, :, :nope].reshape(ql, H * nope).astype(bf16)
        w_qr = w_q_b[l][:, :, nope:].reshape(ql, H * rope).astype(bf16)
        w_ukt = jnp.transpose(w_kv_b[l][:, :, :nope], (1, 2, 0)).astype(bf16)
        w_uv = jnp.transpose(w_kv_b[l][:, :, nope:], (1, 0, 2)).astype(bf16)

        h = rmsnorm(x, ln_mix[l], bf16)
        p = matmul(h, w_in_l)
        (sbq, mbq, qlat, qpe, sb_k, sb_v, ckv, kpe, mb_k, mb_v) = postproj(
            p, cos64, sin64, q_a_norm[l], kv_a_norm[l], w_qn, w_qr, w_ukt, dims)

        o_sb_p = sb_prompt(sbq, sb_k, sb_v, B, T)
        o_lat_p = mla_prompt(qlat, qpe, ckv, kpe, B, T, mla_scale)
        o_mb_p = moba_prompt(mbq, mb_k, mb_v, slopes, B, T)

        o_sb_s = sb_sample(to_seq(sbq[:, Mp:]), new_pad(sb_k[Mp:]), new_pad(sb_v[Mp:]),
                           sbk_t, sbv_t, page_table, l, DT)
        o_lat_s = mla_sample(to_seq(qlat[:, Mp:]), to_seq(qpe[:, Mp:]), new_pad(ckv[Mp:]), new_pad(kpe[Mp:]),
                             cache_mla_ckv, kpe_t, page_table, l, DT, mla_scale)
        o_mb_s = moba_sample(to_seq(mbq[:, Mp:]), new_pad(mb_k[Mp:]), new_pad(mb_v[Mp:]), slopes,
                             mbk_t, mbv_t, page_table, l, DT)

        o_sb = jnp.concatenate([o_sb_p, from_seq_rows(o_sb_s)], axis=0)
        o_lat = jnp.concatenate([o_lat_p, from_seq_heads(o_lat_s)], axis=1)
        o_mb = jnp.concatenate([o_mb_p, from_seq_rows(o_mb_s)], axis=0)
        cat = mixnorm(o_sb, o_lat, o_mb, w_uv, g_mix[l])
        x = matmul(cat, w_out[l].astype(bf16), residual=x)

        i = l // 2
        if l % 2 == 0:
            h2 = rmsnorm(x, ln_ff[l], bf16)
            a = gateup(h2, w_ff_gate[i], w_ff_up[i])
            x = matmul(a, w_ff_down[i].astype(bf16), residual=x)
        else:
            h2, info = router(x, ln_ff[l], w_router[i])
            n_exp = w_router.shape[2]
            pos, src_tok, tile_expert, n_used, n_rows = _dispatch_plan(info[:, :TOP_K].astype(jnp.int32), n_exp)
            hs = gather_rows(h2, src_tok, n_used, n_rows)
            a = moe_gateup(hs, w_ex_gate[i], w_ex_up[i], tile_expert, n_used)
            y = moe_down(a, w_ex_down[i], tile_expert, n_used)
            x = moe_combine(x, y, pos, info[:, TOP_K:2 * TOP_K])

        for idx, arr in enumerate((sb_k, sb_v, ckv, kpe, mb_k, mb_v)):
            rows_p[idx].append(arr[:Mp].reshape(B, T, arr.shape[-1]))
            rows_s[idx].append(arr[Mp:].reshape(DB, DT, arr.shape[-1]))

    y = rmsnorm(x, ln_final, f32)
    y_prompt = y[:Mp].reshape(B, T, D)
    y_sample = y[Mp:].reshape(DB, DT, D)
    return (y_prompt, y_sample, *[jnp.stack(r, 0) for r in rows_p], *[jnp.stack(r, 0) for r in rows_s])
```
